```python
import math
import jax
import jax.numpy as jnp
from jax import lax
import numpy as np

D_MODEL = 1024
BATCH = 8
SEQ = 4096
DEPTH = 4

N_EVEN = (DEPTH + 1) // 2
N_ODD = DEPTH // 2
SB_HEADS = 8
SB_HEAD_DIM = D_MODEL // 16
SB_WIDTH = SB_HEADS * SB_HEAD_DIM
Q_BLOCK = 128
ML_HEADS = 4
ML_HEAD_DIM = D_MODEL // 8
ML_WIDTH = ML_HEADS * ML_HEAD_DIM
ML_CHUNK = 64
CONV_WIDTH = 4
MIX_WIDTH = SB_WIDTH + ML_WIDTH
SPLITS = (SB_WIDTH, 2 * SB_WIDTH, 3 * SB_WIDTH, 3 * SB_WIDTH + 2 * ML_WIDTH,
          3 * SB_WIDTH + 3 * ML_WIDTH, 3 * SB_WIDTH + 4 * ML_WIDTH)
IN_COLS = 3 * SB_WIDTH + 4 * ML_WIDTH + 2 * ML_HEADS
RW_HEAD_DIM = 64
RW_HEADS = D_MODEL // RW_HEAD_DIM
DECAY_LORA = 64
AAA_LORA = 64
GATE_LORA = 128
D_FF = 4 * D_MODEL
NORM_EPS = 1e-6
GN_EPS = 64e-5

kernel_name = 'hybrid_stickbreak_mlstm_rwkv7'


def rmsnorm(x, g):
    xf = x.astype(jnp.float32)
    y = xf * lax.rsqrt(jnp.mean(xf * xf, axis=-1, keepdims=True) + NORM_EPS) * g
    return y.astype(x.dtype)


def causal_depthwise_conv(x, w):
    K, C = w.shape
    return lax.conv_general_dilated(
        x, w.astype(x.dtype)[:, None, :], window_strides=(1,), padding=[(K - 1, 0)],
        dimension_numbers=('NWC', 'WIO', 'NWC'), feature_group_count=C)


def stick_breaking_attention(q, k, v):
    S, d = q.shape[2], q.shape[3]
    scale = 1.0 / math.sqrt(d)
    outs = []
    for blk in range(S // Q_BLOCK):
        q0 = blk * Q_BLOCK
        q1 = q0 + Q_BLOCK
        qb = q[:, :, q0:q1]
        kb = k[:, :, :q1]
        vb = v[:, :, :q1]
        z = jnp.einsum('bhtd,bhsd->bhts', qb, kb) * scale
        t_idx = q0 + jnp.arange(Q_BLOCK)[:, None]
        s_idx = jnp.arange(q1)[None, :]
        causal = s_idx < t_idx
        log_keep = jnp.where(causal, jax.nn.log_sigmoid(-z), 0.0)
        suffix = lax.cumsum(log_keep, axis=3, reverse=True) - log_keep
        weights = jnp.where(causal, jnp.exp(jax.nn.log_sigmoid(z) + suffix), 0.0)
        outs.append(jnp.einsum('bhts,bhsd->bhtd', weights, vb))
    return jnp.concatenate(outs, axis=2)


def mlstm_chunkwise(q, k, v, log_i, log_f):
    B, H, S, d = q.shape
    nc = S // ML_CHUNK
    k = k * (d ** -0.5)
    tril = jnp.tril(jnp.ones((ML_CHUNK, ML_CHUNK), dtype=bool))

    def to_chunks(t):
        t = t.reshape(t.shape[:2] + (nc, ML_CHUNK) + t.shape[3:])
        return jnp.moveaxis(t, 2, 0)

    def step(carry, inp):
        C, n, m = carry
        qc, kc, vc, li, lf = inp
        b = jnp.cumsum(lf, axis=-1)
        dmat = jnp.where(tril, b[..., :, None] - b[..., None, :] + li[..., None, :], -jnp.inf)
        inter = b + m[..., None]
        m_t = jnp.maximum(jnp.max(dmat, axis=-1), inter)
        scores = jnp.einsum('bhtd,bhsd->bhts', qc, kc) * jnp.exp(dmat - m_t[..., None])
        w_inter = jnp.exp(inter - m_t)
        num = jnp.einsum('bhts,bhsd->bhtd', scores, vc) + \
            w_inter[..., None] * jnp.einsum('bhtd,bhde->bhte', qc, C)
        den = jnp.sum(scores, axis=-1) + w_inter * jnp.einsum('bhtd,bhd->bht', qc, n)
        h = num / jnp.maximum(jnp.abs(den), jnp.exp(-m_t))[..., None]
        b_last = b[..., -1]
        g = b_last[..., None] - b + li
        m_new = jnp.maximum(b_last + m, jnp.max(g, axis=-1))
        w_state = jnp.exp(b_last + m - m_new)
        w_tok = jnp.exp(g - m_new[..., None])
        C_new = w_state[..., None, None] * C + jnp.einsum('bhs,bhsd,bhse->bhde', w_tok, kc, vc)
        n_new = w_state[..., None] * n + jnp.einsum('bhs,bhsd->bhd', w_tok, kc)
        return (C_new, n_new, m_new), h

    init = (jnp.zeros((B, H, d, d), jnp.float32), jnp.zeros((B, H, d), jnp.float32),
            jnp.zeros((B, H), jnp.float32))
    _, h = lax.scan(step, init, (to_chunks(q), to_chunks(k), to_chunks(v),
                                 to_chunks(log_i), to_chunks(log_f)))
    return jnp.moveaxis(h, 0, 2).reshape(B, H, S, d)


def stickbreak_mlstm_mix(u, w_in, b_if, conv_w, head_g, w_out):
    B, S, _ = u.shape
    proj = u.astype(jnp.float32) @ w_in
    sb_q, sb_k, sb_v, ml_qk, ml_v, ml_o, ml_if = jnp.split(proj, SPLITS, axis=-1)

    def heads(t, n_heads):
        return t.reshape(B, S, n_heads, -1).transpose(0, 2, 1, 3)

    a_out = stick_breaking_attention(heads(sb_q, SB_HEADS), heads(sb_k, SB_HEADS),
                                     heads(sb_v, SB_HEADS))
    a_out = a_out.transpose(0, 2, 1, 3).reshape(B, S, SB_WIDTH)
    ml_qk = jax.nn.silu(causal_depthwise_conv(ml_qk, conv_w))
    ml_q, ml_k = jnp.split(ml_qk, 2, axis=-1)
    ml_if = ml_if + b_if
    log_i = ml_if[..., :ML_HEADS].transpose(0, 2, 1)
    log_f = jax.nn.log_sigmoid(ml_if[..., ML_HEADS:]).transpose(0, 2, 1)
    h = mlstm_chunkwise(heads(ml_q, ML_HEADS), heads(ml_k, ML_HEADS), heads(ml_v, ML_HEADS),
                        log_i, log_f)
    h = h.transpose(0, 2, 1, 3)
    h = h * lax.rsqrt(jnp.mean(h * h, axis=-1, keepdims=True) + NORM_EPS) * head_g
    h = h.reshape(B, S, ML_WIDTH) * jax.nn.sigmoid(ml_o)
    return jnp.concatenate([a_out, h], axis=-1) @ w_out


def rwkv7_time_mix(u, mu, w_rkv, w0, w1, w2, a0, a1, a2, g1, g2, k_k, k_a, r_k,
                   ln_g, ln_b, w_out):
    B, S, D = u.shape
    xf = u.astype(jnp.float32)
    x_prev = jnp.pad(xf, ((0, 0), (1, 0), (0, 0)))[:, :S]
    xx = x_prev - xf
    xr, xw, xk, xv, xa, xg = (xf + xx * mu[i] for i in range(6))
    r = xr @ w_rkv[0]
    k = xk @ w_rkv[1]
    v = xv @ w_rkv[2]
    log_w = -jax.nn.softplus(-(w0 + jnp.tanh(xw @ w1) @ w2)) - 0.5
    decay = jnp.exp(-jnp.exp(log_w))
    a = jax.nn.sigmoid(a0 + (xa @ a1) @ a2)
    gate = jax.nn.sigmoid(xg @ g1) @ g2

    def heads(t):
        return t.reshape(B, S, RW_HEADS, RW_HEAD_DIM)

    kk = heads(k * k_k)
    kk = kk * lax.rsqrt(jnp.maximum(jnp.sum(kk * kk, axis=-1, keepdims=True), 1e-24))
    k = k * (1.0 + (a - 1.0) * k_a)
    r, k, v, decay, a = heads(r), heads(k), heads(v), heads(decay), heads(a)

    def step(state, inp):
        r_t, w_t, k_t, v_t, kk_t, a_t = inp
        sa = jnp.einsum('bhvk,bhk->bhv', state, kk_t)
        state = state * w_t[:, :, None, :] - sa[..., None] * (kk_t * a_t)[:, :, None, :] \
            + v_t[..., None] * k_t[:, :, None, :]
        return state, jnp.einsum('bhvk,bhk->bhv', state, r_t)

    def time_major(t):
        return jnp.moveaxis(t, 1, 0)

    init = jnp.zeros((B, RW_HEADS, RW_HEAD_DIM, RW_HEAD_DIM), jnp.float32)
    _, y = lax.scan(step, init, (time_major(r), time_major(decay), time_major(k),
                                 time_major(v), time_major(kk), time_major(a)))
    y = jnp.moveaxis(y, 0, 1)
    mean = jnp.mean(y, axis=-1, keepdims=True)
    var = jnp.mean(jnp.square(y - mean), axis=-1, keepdims=True)
    y = (y - mean) * lax.rsqrt(var + GN_EPS) * ln_g + ln_b
    y = y + jnp.sum(r * k * r_k, axis=-1, keepdims=True) * v
    return (y.reshape(B, S, D) * gate) @ w_out


def squared_relu_mlp(u, w_up, w_down):
    return jnp.square(jax.nn.relu(u.astype(jnp.float32) @ w_up)) @ w_down


def setup_inputs(seed: int = 0) -> dict:
    key = jax.random.key(seed)
    ks = iter(jax.random.split(key, 32))
    D = D_MODEL

    def nrm(shape, scale):
        return jax.random.normal(next(ks), shape, jnp.float32) * scale

    def uni(shape, lo, hi):
        return jax.random.uniform(next(ks), shape, jnp.float32, lo, hi)

    x = nrm((BATCH, SEQ, D), 1.0)
    norm_g = 1.0 + nrm((DEPTH, 4, D), 0.02)
    e_w_in = nrm((N_EVEN, D, IN_COLS), D ** -0.5)
    e_b_if = jnp.concatenate([nrm((N_EVEN, ML_HEADS), 0.1),
                              uni((N_EVEN, ML_HEADS), 3.0, 6.0)], axis=-1)
    e_conv_w = nrm((N_EVEN, CONV_WIDTH, 2 * ML_WIDTH), CONV_WIDTH ** -0.5)
    e_head_g = 1.0 + nrm((N_EVEN, ML_HEADS, ML_HEAD_DIM), 0.02)
    e_w_out = nrm((N_EVEN, MIX_WIDTH, D), MIX_WIDTH ** -0.5)
    r_mu = uni((N_ODD, 6, D), 0.0, 1.0)
    r_w_rkv = nrm((N_ODD, 3, D, D), D ** -0.5)
    r_w0 = uni((N_ODD, D), -6.0, 1.0)
    r_w1 = nrm((N_ODD, D, DECAY_LORA), D ** -0.5)
    r_w2 = nrm((N_ODD, DECAY_LORA, D), 0.5 * DECAY_LORA ** -0.5)
    r_a0 = nrm((N_ODD, D), 0.1)
    r_a1 = nrm((N_ODD, D, AAA_LORA), D ** -0.5)
    r_a2 = nrm((N_ODD, AAA_LORA, D), 0.5 * AAA_LORA ** -0.5)
    r_g1 = nrm((N_ODD, D, GATE_LORA), D ** -0.5)
    r_g2 = nrm((N_ODD, GATE_LORA, D), GATE_LORA ** -0.5)
    r_k_k = 0.85 + nrm((N_ODD, D), 0.02)
    r_k_a = 1.0 + nrm((N_ODD, D), 0.02)
    r_r_k = nrm((N_ODD, RW_HEADS, RW_HEAD_DIM), 0.1)
    r_ln_g = 1.0 + nrm((N_ODD, RW_HEADS, RW_HEAD_DIM), 0.02)
    r_ln_b = nrm((N_ODD, RW_HEADS, RW_HEAD_DIM), 0.02)
    r_w_out = nrm((N_ODD, D, D), D ** -0.5)
    mlp_w_up = nrm((DEPTH, D, D_FF), D ** -0.5)
    mlp_w_down = nrm((DEPTH, D_FF, D), D_FF ** -0.5)
    return {'x': x, 'norm_g': norm_g, 'e_w_in': e_w_in, 'e_b_if': e_b_if,
            'e_conv_w': e_conv_w, 'e_head_g': e_head_g, 'e_w_out': e_w_out,
            'r_mu': r_mu, 'r_w_rkv': r_w_rkv, 'r_w0': r_w0, 'r_w1': r_w1, 'r_w2': r_w2,
            'r_a0': r_a0, 'r_a1': r_a1, 'r_a2': r_a2, 'r_g1': r_g1, 'r_g2': r_g2,
            'r_k_k': r_k_k, 'r_k_a': r_k_a, 'r_r_k': r_r_k, 'r_ln_g': r_ln_g,
            'r_ln_b': r_ln_b, 'r_w_out': r_w_out, 'mlp_w_up': mlp_w_up,
            'mlp_w_down': mlp_w_down}


def reference(x, norm_g, e_w_in, e_b_if, e_conv_w, e_head_g, e_w_out, r_mu, r_w_rkv,
              r_w0, r_w1, r_w2, r_a0, r_a1, r_a2, r_g1, r_g2, r_k_k, r_k_a, r_r_k,
              r_ln_g, r_ln_b, r_w_out, mlp_w_up, mlp_w_down):
    h = x
    for layer in range(DEPTH):
        g = norm_g[layer]
        u = rmsnorm(h, g[0])
        if layer % 2 == 0:
            e = layer // 2
            mix = stickbreak_mlstm_mix(u, e_w_in[e], e_b_if[e], e_conv_w[e], e_head_g[e],
                                       e_w_out[e])
        else:
            o = layer // 2
            mix = rwkv7_time_mix(u, r_mu[o], r_w_rkv[o], r_w0[o], r_w1[o], r_w2[o], r_a0[o],
                                 r_a1[o], r_a2[o], r_g1[o], r_g2[o], r_k_k[o], r_k_a[o],
                                 r_r_k[o], r_ln_g[o], r_ln_b[o], r_w_out[o])
        h = h + rmsnorm(mix, g[1]).astype(h.dtype)
        u = rmsnorm(h, g[2])
        ff = squared_relu_mlp(u, mlp_w_up[layer], mlp_w_down[layer])
        h = h + rmsnorm(ff, g[3]).astype(h.dtype)
    return h
```

```python
import functools
import math

import jax
import jax.numpy as jnp
from jax import lax
from jax.experimental import pallas as pl
from jax.experimental.pallas import tpu as pltpu

F32 = jnp.float32
BF16 = jnp.bfloat16

LANES = 128
V7X_VMEM_LIMIT_BYTES = 56 * 1024 * 1024

SB_HEADS = 8
SB_HEAD_DIM = 64
SB_WIDTH = SB_HEADS * SB_HEAD_DIM
ML_HEADS = 4
ML_HEAD_DIM = 128
ML_WIDTH = ML_HEADS * ML_HEAD_DIM
CONV_WIDTH = 4
RW_HEAD_DIM = 64
NORM_EPS = 1e-6
GN_EPS = 64e-5

ROW_TILE = 512
RW_ROW_TILE = 256
MLP_ROW_TILE = 1024
MLP_FF_TILE = 1024
SB_Q_TILE = 256
SB_K_TILE = 128
ML_CHUNK = 256
RW_CHUNK = 64
RW_BLOCK = 256
HALO = 8


def _params(*sem):
    return pltpu.CompilerParams(dimension_semantics=sem,
                                vmem_limit_bytes=V7X_VMEM_LIMIT_BYTES)


def _dot(a, b):
    return jnp.dot(a.astype(BF16), b.astype(BF16), preferred_element_type=F32)


def _dot_nt(a, b):
    return lax.dot_general(a.astype(BF16), b.astype(BF16), (((1,), (1,)), ((), ())),
                           preferred_element_type=F32)


def _dot_tn(a, b):
    return lax.dot_general(a.astype(BF16), b.astype(BF16), (((0,), (0,)), ((), ())),
                           preferred_element_type=F32)


def _split(x):
    hi = x.astype(BF16)
    lo = (x - hi.astype(F32)).astype(BF16)
    return hi, lo


def _dot_x_exact(x, m):
    hi, lo = _split(x)
    return (jnp.dot(hi, m, preferred_element_type=F32)
            + jnp.dot(lo, m, preferred_element_type=F32))


def _dot_exact_x(m, x):
    hi, lo = _split(x)
    return (jnp.dot(m, hi, preferred_element_type=F32)
            + jnp.dot(m, lo, preferred_element_type=F32))


def _dot3(a, b):
    ah, al = _split(a)
    bh, bl = _split(b)
    return (jnp.dot(ah, bh, preferred_element_type=F32)
            + jnp.dot(ah, bl, preferred_element_type=F32)
            + jnp.dot(al, bh, preferred_element_type=F32))


def _rms(x, g):
    ms = jnp.mean(x * x, axis=-1, keepdims=True)
    return x * lax.rsqrt(ms + NORM_EPS) * g


def _softplus(z):
    return jnp.maximum(z, 0.0) + jnp.log1p(jnp.exp(-jnp.abs(z)))


def _sigmoid(z):
    return 1.0 / (1.0 + jnp.exp(-z))


def _inproj_kernel(x_ref, g_ref, w_ref, wif_ref, *out_refs, widths):
    u = _rms(x_ref[...], g_ref[...]).astype(BF16)
    c0 = 0
    for o_ref, w in zip(out_refs[:-1], widths):
        o_ref[...] = jnp.dot(u, w_ref[:, c0:c0 + w],
                             preferred_element_type=F32).astype(o_ref.dtype)
        c0 += w
    out_refs[-1][...] = jnp.dot(u, wif_ref[...], preferred_element_type=F32)


def _even_inproj(h, g, w_main, w_if):
    T, D = h.shape
    widths = (SB_WIDTH, SB_WIDTH, SB_WIDTH, 2 * ML_WIDTH, ML_WIDTH, ML_WIDTH)
    tm = ROW_TILE
    out_shape = [jax.ShapeDtypeStruct((T, w), BF16) for w in widths]
    out_shape.append(jax.ShapeDtypeStruct((T, LANES), F32))
    out_specs = [pl.BlockSpec((tm, w), lambda i: (i, 0)) for w in widths]
    out_specs.append(pl.BlockSpec((tm, LANES), lambda i: (i, 0)))
    return pl.pallas_call(
        functools.partial(_inproj_kernel, widths=widths),
        grid=(T // tm,),
        in_specs=[pl.BlockSpec((tm, D), lambda i: (i, 0)),
                  pl.BlockSpec((1, D), lambda i: (0, 0)),
                  pl.BlockSpec(w_main.shape, lambda i: (0, 0)),
                  pl.BlockSpec(w_if.shape, lambda i: (0, 0))],
        out_specs=out_specs,
        out_shape=out_shape,
        compiler_params=_params("parallel"),
        name="even_inproj",
    )(h, g, w_main, w_if)


def _sb_kernel(q_ref, k_ref, v_ref, o_ref, *, tq, tk, scale):
    qi = pl.program_id(2)
    q2 = q_ref[0]
    lane = lax.broadcasted_iota(jnp.int32, (1, LANES), 1)
    first_head = lane < SB_HEAD_DIM
    rr = lax.broadcasted_iota(jnp.int32, (tk, 2 * tk), 0)
    cc = lax.broadcasted_iota(jnp.int32, (tk, 2 * tk), 1)
    cum_mat = jnp.where((cc >= tk) | (rr > cc), 1.0, 0.0).astype(BF16)
    row_t = lax.broadcasted_iota(jnp.int32, (tq, tk), 0) + qi * tq
    col_s = lax.broadcasted_iota(jnp.int32, (tq, tk), 1)

    def tile(qh, kb, carry, acc, masked):
        start = pl.multiple_of(kb * tk, tk)
        ks = k_ref[0, pl.ds(start, tk), :]
        vs = v_ref[0, pl.ds(start, tk), :]
        z = lax.dot_general(qh, ks, (((1,), (1,)), ((), ())), preferred_element_type=F32)
        lk = -_softplus(z)
        if masked:
            causal = (col_s + start) < row_t
            lk = jnp.where(causal, lk, 0.0)
        cs = _dot_x_exact(lk, cum_mat)
        logw = z + lk + cs[:, :tk] + carry
        p = jnp.exp(logw)
        if masked:
            p = jnp.where(causal, p, 0.0)
        acc = acc + jnp.dot(p.astype(BF16), vs, preferred_element_type=F32)
        return carry + cs[:, tk:], acc

    outs = []
    nsub = tq // tk
    for head in range(2):
        keep = first_head if head == 0 else jnp.logical_not(first_head)
        qh = (jnp.where(keep, q2, jnp.zeros_like(q2)).astype(F32) * scale).astype(BF16)
        carry = jnp.zeros((tq, tk), F32)
        acc = jnp.zeros((tq, LANES), F32)
        for sub in range(nsub - 1, -1, -1):
            carry, acc = tile(qh, qi * nsub + sub, carry, acc, True)

        def body(j, c):
            return tile(qh, qi * nsub - 1 - j, c[0], c[1], False)

        carry, acc = lax.fori_loop(0, qi * nsub, body, (carry, acc))
        outs.append(acc)
    o_ref[0] = jnp.where(first_head, outs[0], outs[1]).astype(o_ref.dtype)


def _sb_attention(q, k, v):
    B, S, W = q.shape
    tq, tk = SB_Q_TILE, SB_K_TILE
    npair = W // LANES
    return pl.pallas_call(
        functools.partial(_sb_kernel, tq=tq, tk=tk, scale=1.0 / math.sqrt(SB_HEAD_DIM)),
        grid=(B, npair, S // tq),
        in_specs=[pl.BlockSpec((1, tq, LANES), lambda b, p, i: (b, i, p)),
                  pl.BlockSpec((1, S, LANES), lambda b, p, i: (b, 0, p)),
                  pl.BlockSpec((1, S, LANES), lambda b, p, i: (b, 0, p))],
        out_specs=pl.BlockSpec((1, tq, LANES), lambda b, p, i: (b, i, p)),
        out_shape=jax.ShapeDtypeStruct((B, S, W), BF16),
        compiler_params=_params("parallel", "parallel", "arbitrary"),
        name="sb_attention",
    )(q, k, v)


def _mlstm_kernel(q_ref, k_ref, v_ref, o_ref, gn_ref, gt_ref, bn_ref, bt_ref,
                  cwq_ref, cwk_ref, hg_ref, out_ref,
                  c_ref, m_ref, pq_ref, pk_ref, *, L):
    head = pl.program_id(1)
    chunk = pl.program_id(2)

    @pl.when(chunk == 0)
    def _():
        c_ref[...] = jnp.zeros_like(c_ref)
        m_ref[...] = jnp.zeros_like(m_ref)
        pq_ref[...] = jnp.zeros_like(pq_ref)
        pk_ref[...] = jnp.zeros_like(pk_ref)

    def conv_silu(x, prev_ref, w_ref):
        xf = jnp.concatenate([prev_ref[...], x], axis=0)
        w = w_ref[0]
        y = xf * w[CONV_WIDTH - 1:CONV_WIDTH, :]
        for j in range(1, CONV_WIDTH):
            y = y + pltpu.roll(xf, j, 0) * w[CONV_WIDTH - 1 - j:CONV_WIDTH - j, :]
        prev_ref[...] = x[L - HALO:, :]
        y = y[HALO:, :]
        return y * _sigmoid(y)

    q = conv_silu(q_ref[0].astype(F32), pq_ref, cwq_ref)
    k = conv_silu(k_ref[0].astype(F32), pk_ref, cwk_ref) * (ML_HEAD_DIM ** -0.5)
    v = v_ref[0]

    lane = lax.broadcasted_iota(jnp.int32, (1, LANES), 1)
    gn = gn_ref[0] + bn_ref[...]
    li_col = jnp.sum(jnp.where(lane == head, gn, 0.0), axis=1, keepdims=True)
    f_col = jnp.sum(jnp.where(lane == head + ML_HEADS, gn, 0.0), axis=1, keepdims=True)
    li_row = gt_ref[0, pl.ds(head, 1), :] + bt_ref[pl.ds(head, 1), :]
    f_row = gt_ref[0, pl.ds(head + ML_HEADS, 1), :] + bt_ref[pl.ds(head + ML_HEADS, 1), :]
    lf_col = -_softplus(-f_col)
    lf_row = -_softplus(-f_row)

    ri = lax.broadcasted_iota(jnp.int32, (L, L), 0)
    ci = lax.broadcasted_iota(jnp.int32, (L, L), 1)
    tril = ci <= ri
    tril_m = jnp.where(tril, 1.0, 0.0).astype(BF16)
    triu_m = jnp.where(ci >= ri, 1.0, 0.0).astype(BF16)
    b_col = _dot_exact_x(tril_m, jnp.broadcast_to(lf_col, (L, LANES)))[:, :1]
    b_row = _dot_x_exact(jnp.broadcast_to(lf_row, (HALO, L)), triu_m)[:1, :]

    m_prev = m_ref[0:1, 0:1]
    dmat = jnp.where(tril, b_col - b_row + li_row, -jnp.inf)
    inter = b_col + m_prev
    m_t = jnp.maximum(jnp.max(dmat, axis=1, keepdims=True), inter)
    scores = _dot_nt(q, k) * jnp.exp(dmat - m_t)
    w_inter = jnp.exp(inter - m_t)
    v_aug = jnp.concatenate([v, jnp.ones_like(v)], axis=1)
    c_aug = c_ref[...]
    num_aug = _dot(scores, v_aug) + w_inter * _dot(q, c_aug)
    num = num_aug[:, :LANES]
    den = num_aug[:, LANES:LANES + 1]
    hval = num / jnp.maximum(jnp.abs(den), jnp.exp(-m_t))

    b_last = b_col[L - 1:L, :]
    gcol = b_last - b_col + li_col
    m_new = jnp.maximum(b_last + m_prev, jnp.max(gcol, axis=0, keepdims=True))
    w_state = jnp.exp(b_last + m_prev - m_new)
    w_tok = jnp.exp(gcol - m_new)
    c_ref[...] = w_state * c_aug + _dot_tn(k * w_tok, v_aug)
    m_ref[...] = jnp.broadcast_to(m_new, m_ref.shape)

    hn = hval * lax.rsqrt(jnp.mean(hval * hval, axis=-1, keepdims=True) + NORM_EPS) * hg_ref[0]
    out_ref[0] = (hn * _sigmoid(o_ref[0].astype(F32))).astype(out_ref.dtype)


def _mlstm(qk, v, o, gates_nat, gates_t, bias_nat, bias_t, conv_w, head_g):
    B, S, _ = v.shape
    L = ML_CHUNK
    blk = lambda off: pl.BlockSpec((1, L, LANES), lambda b, h, c: (b, c, h + off))
    return pl.pallas_call(
        functools.partial(_mlstm_kernel, L=L),
        grid=(B, ML_HEADS, S // L),
        in_specs=[blk(0), blk(ML_HEADS), blk(0), blk(0),
                  pl.BlockSpec((1, L, LANES), lambda b, h, c: (b, c, 0)),
                  pl.BlockSpec((1, 2 * ML_HEADS, L), lambda b, h, c: (b, 0, c)),
                  pl.BlockSpec((1, LANES), lambda b, h, c: (0, 0)),
                  pl.BlockSpec((2 * ML_HEADS, L), lambda b, h, c: (0, 0)),
                  pl.BlockSpec((1, CONV_WIDTH, LANES), lambda b, h, c: (h, 0, 0)),
                  pl.BlockSpec((1, CONV_WIDTH, LANES), lambda b, h, c: (h + ML_HEADS, 0, 0)),
                  pl.BlockSpec((1, 1, LANES), lambda b, h, c: (h, 0, 0))],
        out_specs=pl.BlockSpec((1, L, LANES), lambda b, h, c: (b, c, h)),
        out_shape=jax.ShapeDtypeStruct((B, S, ML_WIDTH), BF16),
        scratch_shapes=[pltpu.VMEM((ML_HEAD_DIM, 2 * LANES), F32),
                        pltpu.VMEM((HALO, LANES), F32),
                        pltpu.VMEM((HALO, LANES), F32),
                        pltpu.VMEM((HALO, LANES), F32)],
        compiler_params=_params("parallel", "parallel", "arbitrary"),
        name="mlstm",
    )(qk, qk, v, o, gates_nat, gates_t, bias_nat, bias_t, conv_w, conv_w, head_g)


def _mixout_kernel(*refs, nparts):
    parts = refs[:nparts]
    w_ref, h_ref, g_ref, o_ref = refs[nparts:]
    c0 = 0
    mix = None
    for p in parts:
        w = p.shape[1]
        t = jnp.dot(p[...], w_ref[c0:c0 + w, :], preferred_element_type=F32)
        mix = t if mix is None else mix + t
        c0 += w
    o_ref[...] = h_ref[...] + _rms(mix, g_ref[...])


def _mix_out(parts, w_out, h, g):
    T, D = h.shape
    tm = ROW_TILE
    in_specs = [pl.BlockSpec((tm, p.shape[1]), lambda i: (i, 0)) for p in parts]
    in_specs += [pl.BlockSpec(w_out.shape, lambda i: (0, 0)),
                 pl.BlockSpec((tm, D), lambda i: (i, 0)),
                 pl.BlockSpec((1, D), lambda i: (0, 0))]
    return pl.pallas_call(
        functools.partial(_mixout_kernel, nparts=len(parts)),
        grid=(T // tm,),
        in_specs=in_specs,
        out_specs=pl.BlockSpec((tm, D), lambda i: (i, 0)),
        out_shape=jax.ShapeDtypeStruct((T, D), F32),
        compiler_params=_params("parallel"),
        name="mix_out",
    )(*parts, w_out, h, g)


def _mlp_kernel(h_ref, g2_ref, wu_ref, wd_ref, g3_ref, o_ref, u_ref, acc_ref):
    j = pl.program_id(1)

    @pl.when(j == 0)
    def _():
        u_ref[...] = _rms(h_ref[...], g2_ref[...]).astype(BF16)
        acc_ref[...] = jnp.zeros_like(acc_ref)

    a = jnp.dot(u_ref[...], wu_ref[...], preferred_element_type=F32)
    a = jnp.maximum(a, 0.0)
    acc_ref[...] += jnp.dot((a * a).astype(BF16), wd_ref[...], preferred_element_type=F32)

    @pl.when(j == pl.num_programs(1) - 1)
    def _():
        o_ref[...] = h_ref[...] + _rms(acc_ref[...], g3_ref[...])


def _mlp(h, g2, w_up, w_down, g3):
    T, D = h.shape
    FF = w_up.shape[1]
    tm, tf = MLP_ROW_TILE, MLP_FF_TILE
    return pl.pallas_call(
        _mlp_kernel,
        grid=(T // tm, FF // tf),
        in_specs=[pl.BlockSpec((tm, D), lambda i, j: (i, 0)),
                  pl.BlockSpec((1, D), lambda i, j: (0, 0)),
                  pl.BlockSpec((D, tf), lambda i, j: (0, j)),
                  pl.BlockSpec((tf, D), lambda i, j: (j, 0)),
                  pl.BlockSpec((1, D), lambda i, j: (0, 0))],
        out_specs=pl.BlockSpec((tm, D), lambda i, j: (i, 0)),
        out_shape=jax.ShapeDtypeStruct((T, D), F32),
        scratch_shapes=[pltpu.VMEM((tm, D), BF16), pltpu.VMEM((tm, D), F32)],
        compiler_params=_params("parallel", "arbitrary"),
        name="mlp",
    )(h, g2, w_up, w_down, g3)


def _rwproj_kernel(x_ref, xh_ref, g_ref, mu_ref, wr_ref, wk_ref, wv_ref,
                   w1_ref, w2_ref, a1_ref, a2_ref, g1_ref, g2_ref, vec_ref,
                   r_ref, lw_ref, k_ref, v_ref, kk_ref, a_ref, gate_ref, *, tm, seq):
    i = pl.program_id(0)
    g = g_ref[...]
    u = _rms(x_ref[...], g)
    not_first = jnp.where((i * tm) % seq == 0, 0.0, 1.0)
    up = _rms(xh_ref[HALO - 1:HALO, :], g) * not_first
    row = lax.broadcasted_iota(jnp.int32, (tm, 1), 0)
    x_prev = jnp.where(row == 0, up, pltpu.roll(u, 1, 0))
    xx = x_prev - u
    mix = lambda n: (u + xx * mu_ref[n:n + 1, :]).astype(BF16)
    w0, a0, k_k, k_a = (vec_ref[n:n + 1, :] for n in range(4))

    r = jnp.dot(mix(0), wr_ref[...], preferred_element_type=F32)
    k = jnp.dot(mix(2), wk_ref[...], preferred_element_type=F32)
    v = jnp.dot(mix(3), wv_ref[...], preferred_element_type=F32)
    dw = _dot(jnp.tanh(jnp.dot(mix(1), w1_ref[...], preferred_element_type=F32)), w2_ref[...])
    log_w = -_softplus(-(w0 + dw)) - 0.5
    a = _sigmoid(a0 + _dot(jnp.dot(mix(4), a1_ref[...], preferred_element_type=F32), a2_ref[...]))
    gate = _dot(_sigmoid(jnp.dot(mix(5), g1_ref[...], preferred_element_type=F32)), g2_ref[...])

    r_ref[...] = r
    lw_ref[...] = -jnp.exp(log_w)
    k_ref[...] = k * (1.0 + (a - 1.0) * k_a)
    v_ref[...] = v
    kk_ref[...] = k * k_k
    a_ref[...] = a
    gate_ref[...] = gate


def _rwkv_proj(h, g, mu, wr, wk, wv, w1, w2, a1, a2, g1, g2, vecs, seq):
    T, D = h.shape
    tm = RW_ROW_TILE
    full = lambda arr: pl.BlockSpec(arr.shape, lambda i: (0, 0))
    row = pl.BlockSpec((tm, D), lambda i: (i, 0))
    halo = pl.BlockSpec((HALO, D), lambda i: (jnp.maximum(i * (tm // HALO) - 1, 0), 0))
    return pl.pallas_call(
        functools.partial(_rwproj_kernel, tm=tm, seq=seq),
        grid=(T // tm,),
        in_specs=[row, halo, full(g), full(mu), full(wr), full(wk), full(wv),
                  full(w1), full(w2), full(a1), full(a2), full(g1), full(g2), full(vecs)],
        out_specs=[row] * 7,
        out_shape=[jax.ShapeDtypeStruct((T, D), F32)] * 7,
        compiler_params=_params("parallel"),
        name="rwkv_proj",
    )(h, h, g, mu, wr, wk, wv, w1, w2, a1, a2, g1, g2, vecs)


def _bmm(a, b):
    return lax.dot_general(a.astype(BF16), b.astype(BF16), (((2,), (1,)), ((0,), (0,))),
                           preferred_element_type=F32)


def _bmm_nt(a, b):
    return lax.dot_general(a.astype(BF16), b.astype(BF16), (((2,), (2,)), ((0,), (0,))),
                           preferred_element_type=F32)


def _bmm_tn(a, b):
    return lax.dot_general(a.astype(BF16), b.astype(BF16), (((1,), (1,)), ((0,), (0,))),
                           preferred_element_type=F32)


def _rwscan_kernel(r_ref, lw_ref, k_ref, v_ref, kk_ref, a_ref, gate_ref,
                   rk_ref, lng_ref, lnb_ref, o_ref, s_ref, *, L, nsub):
    @pl.when(pl.program_id(2) == 0)
    def _():
        s_ref[...] = jnp.zeros_like(s_ref)

    P = 2 * L
    lane = lax.broadcasted_iota(jnp.int32, (1, 1, LANES), 2)
    m0 = jnp.where(lane < RW_HEAD_DIM, 1.0, 0.0)
    m1 = 1.0 - m0
    ri = lax.broadcasted_iota(jnp.int32, (P, P), 0)
    ci = lax.broadcasted_iota(jnp.int32, (P, P), 1)
    same = (ri >> 6) == (ci >> 6)
    gsum = jnp.where(same, 1.0, 0.0).astype(BF16)
    strict = same & (ci < ri)
    incl = same & (ci <= ri)
    ri_l = lax.broadcasted_iota(jnp.int32, (L, L), 0)
    ci_l = lax.broadcasted_iota(jnp.int32, (L, L), 1)
    tril_l = jnp.where(ci_l <= ri_l, 1.0, 0.0).astype(BF16)

    blk = lambda ref: ref[0].reshape(nsub, L, LANES)
    r, lw, k, v, kk, a = (blk(x) for x in (r_ref, lw_ref, k_ref, v_ref, kk_ref, a_ref))

    def lane_group_sum(x):
        return _dot_x_exact(x, gsum)

    ssq = lane_group_sum((kk * kk).reshape(nsub * L, LANES)).reshape(nsub, L, LANES)
    kap = kk * lax.rsqrt(jnp.maximum(ssq, 1e-24))
    bv = kap * a

    t1 = lw.astype(BF16)
    r1 = lw - t1.astype(F32)
    t2 = r1.astype(BF16)
    t3 = (r1 - t2.astype(F32)).astype(BF16)
    tril_b = jnp.broadcast_to(tril_l, (nsub, L, L))
    cum = (_bmm(tril_b, t1) + _bmm(tril_b, t2)) + _bmm(tril_b, t3)
    c_last = cum[:, L - 1:L, :]
    w_incl = jnp.exp(cum)
    w_excl = jnp.exp(cum - lw)
    w_inv = jnp.exp(-cum)
    w_rem = jnp.exp(c_last - cum)
    w_last = jnp.exp(c_last)

    stack = lambda x: jnp.concatenate([x * m0, x * m1], axis=1)
    dup = lambda x: jnp.concatenate([x, x], axis=1)
    unstack = lambda x: x[:, :L, :] + x[:, L:, :]

    a_st = stack(-kap * w_excl)
    r_st = stack(r * w_incl)
    v_st = stack(v)
    gram = _bmm_nt(jnp.concatenate([a_st, r_st], axis=1),
                   jnp.concatenate([dup(bv * w_inv), dup(k * w_inv)], axis=1))
    a_ab = jnp.where(strict, gram[:, :P, :P], 0.0)
    a_ak = jnp.where(strict, gram[:, :P, P:], 0.0)
    a_rb = jnp.where(incl, gram[:, P:, :P], 0.0)
    a_rk = jnp.where(incl, gram[:, P:, P:], 0.0)

    tinv = jnp.where(ri == ci, 1.0, 0.0) + jnp.where((ri >> 1) == (ci >> 1), a_ab, 0.0)
    n = 2
    while n < L:
        sh = n.bit_length() - 1
        off = ((ri >> (sh + 1)) == (ci >> (sh + 1))) & ((ri >> sh) != (ci >> sh))
        e = jnp.where(off, a_ab, 0.0)
        tinv = tinv + _bmm(tinv, _bmm(e, tinv))
        n *= 2

    x1 = _bmm(a_ak, v_st)
    tu = _bmm(tinv, jnp.concatenate([x1, a_st], axis=2))
    u0_st, ta_st = tu[:, :, :LANES], tu[:, :, LANES:]
    yr = _bmm(a_rb, tu)
    y0 = unstack(yr[:, :, :LANES] + _bmm(a_rk, v_st))
    rq = unstack(r_st + yr[:, :, LANES:])
    bh_st = stack(bv * w_rem)
    kh_st = stack(k * w_rem)
    mn = _bmm_tn(bh_st, jnp.concatenate([ta_st, u0_st], axis=2))
    m_bd = mn[:, :, :LANES] + jnp.where(ri == ci, w_last, 0.0)
    n_bd = mn[:, :, LANES:] + _bmm_tn(kh_st, v_st)

    s = s_ref[...]
    ys = []
    for c in range(nsub):
        ys.append(_dot(rq[c], s) + y0[c])
        s = _dot3(m_bd[c], s) + n_bd[c]
    s_ref[...] = s
    y = jnp.concatenate(ys, axis=0)

    inv_d = 1.0 / RW_HEAD_DIM
    mean = lane_group_sum(y) * inv_d
    d = y - mean
    var = lane_group_sum(d * d) * inv_d
    yn = d * lax.rsqrt(var + GN_EPS) * lng_ref[...] + lnb_ref[...]
    r2, k2, v2 = r_ref[0], k_ref[0], v_ref[0]
    bonus = lane_group_sum(r2 * k2 * rk_ref[...]) * v2
    o_ref[0] = ((yn + bonus) * gate_ref[0]).astype(o_ref.dtype)


def _rwkv_scan(r, lw, k, v, kk, a, gate, r_k, ln_g, ln_b):
    B, S, D = r.shape
    L, lb = RW_CHUNK, RW_BLOCK
    blk = pl.BlockSpec((1, lb, LANES), lambda b, p, j: (b, j, p))
    vec = pl.BlockSpec((1, LANES), lambda b, p, j: (0, p))
    return pl.pallas_call(
        functools.partial(_rwscan_kernel, L=L, nsub=lb // L),
        grid=(B, D // LANES, S // lb),
        in_specs=[blk] * 7 + [vec] * 3,
        out_specs=blk,
        out_shape=jax.ShapeDtypeStruct((B, S, D), BF16),
        scratch_shapes=[pltpu.VMEM((LANES, LANES), F32)],
        compiler_params=_params("parallel", "parallel", "arbitrary"),
        name="rwkv_scan",
    )(r, lw, k, v, kk, a, gate, r_k, ln_g, ln_b)


def _even_layer(h, B, S, g, w_in, b_if, conv_w, head_g, w_out):
    T, D = h.shape
    ncols = 3 * SB_WIDTH + 4 * ML_WIDTH
    w_main = w_in[:, :ncols].astype(BF16)
    w_if = jnp.pad(w_in[:, ncols:], ((0, 0), (0, LANES - 2 * ML_HEADS))).astype(BF16)
    sb_q, sb_k, sb_v, ml_qk, ml_v, ml_o, ml_if = _even_inproj(h, g[0:1], w_main, w_if)
    to3 = lambda t: t.reshape(B, S, t.shape[-1])
    a_out = _sb_attention(to3(sb_q), to3(sb_k), to3(sb_v))
    gates_nat = to3(ml_if)
    gates_t = jnp.swapaxes(gates_nat[:, :, :2 * ML_HEADS], 1, 2)
    bias_nat = jnp.pad(b_if, (0, LANES - 2 * ML_HEADS)).reshape(1, LANES)
    bias_t = jnp.broadcast_to(b_if[:, None], (2 * ML_HEADS, ML_CHUNK))
    cw = conv_w.reshape(CONV_WIDTH, 2 * ML_HEADS, LANES).transpose(1, 0, 2)
    hm = _mlstm(to3(ml_qk), to3(ml_v), to3(ml_o), gates_nat, gates_t, bias_nat, bias_t,
                cw, head_g.reshape(ML_HEADS, 1, LANES))
    return _mix_out([a_out.reshape(T, SB_WIDTH), hm.reshape(T, ML_WIDTH)],
                    w_out.astype(BF16), h, g[1:2])


def _odd_layer(h, B, S, g, mu, w_rkv, w0, w1, w2, a0, a1, a2, g1, g2, k_k, k_a, r_k,
               ln_g, ln_b, w_out):
    T, D = h.shape
    bf = lambda t: t.astype(BF16)
    mu8 = jnp.pad(mu, ((0, HALO - mu.shape[0]), (0, 0)))
    vecs = jnp.pad(jnp.stack([w0, a0, k_k, k_a]), ((0, HALO - 4), (0, 0)))
    r, lw, k, v, kk, a, gate = _rwkv_proj(
        h, g[0:1], mu8, bf(w_rkv[0]), bf(w_rkv[1]), bf(w_rkv[2]),
        bf(w1), bf(w2), bf(a1), bf(a2), bf(g1), bf(g2), vecs, S)
    to3 = lambda t: t.reshape(B, S, D)
    y = _rwkv_scan(to3(r), to3(lw), to3(k), to3(v), to3(kk), to3(a), to3(gate),
                   r_k.reshape(1, D), ln_g.reshape(1, D), ln_b.reshape(1, D))
    return _mix_out([y.reshape(T, D)], bf(w_out), h, g[1:2])


def kernel(x, norm_g, e_w_in, e_b_if, e_conv_w, e_head_g, e_w_out, r_mu, r_w_rkv, r_w0, r_w1, r_w2, r_a0, r_a1, r_a2, r_g1, r_g2, r_k_k, r_k_a, r_r_k, r_ln_g, r_ln_b, r_w_out, mlp_w_up, mlp_w_down):
    B, S, D = x.shape
    h = x.reshape(B * S, D)
    for layer in range(norm_g.shape[0]):
        g = norm_g[layer]
        if layer % 2 == 0:
            e = layer // 2
            h = _even_layer(h, B, S, g, e_w_in[e], e_b_if[e], e_conv_w[e], e_head_g[e],
                            e_w_out[e])
        else:
            o = layer // 2
            h = _odd_layer(h, B, S, g, r_mu[o], r_w_rkv[o], r_w0[o], r_w1[o], r_w2[o],
                           r_a0[o], r_a1[o], r_a2[o], r_g1[o], r_g2[o], r_k_k[o], r_k_a[o],
                           r_r_k[o], r_ln_g[o], r_ln_b[o], r_w_out[o])
        h = _mlp(h, g[2:3], mlp_w_up[layer].astype(BF16), mlp_w_down[layer].astype(BF16),
                 g[3:4])
    return h.reshape(B, S, D)
```

```python
import functools
import math

import jax
import jax.numpy as jnp
from jax import lax
from jax.experimental import pallas as pl
from jax.experimental.pallas import tpu as pltpu

F32 = jnp.float32
BF16 = jnp.bfloat16

LANES = 128
V7X_VMEM_LIMIT_BYTES = 56 * 1024 * 1024

SB_HEADS = 8
SB_HEAD_DIM = 64
SB_WIDTH = SB_HEADS * SB_HEAD_DIM
ML_HEADS = 4
ML_HEAD_DIM = 128
ML_WIDTH = ML_HEADS * ML_HEAD_DIM
CONV_WIDTH = 4
RW_HEAD_DIM = 64
SB_Q_SCALE = -math.log2(math.e) / math.sqrt(SB_HEAD_DIM)
NORM_EPS = 1e-6
GN_EPS = 64e-5

ROW_TILE = 512
RW_ROW_TILE = 256
MLP_ROW_TILE = 1024
MLP_FF_TILE = 1024
SB_Q_TILE = 512
SB_K_TILE = 256
ML_CHUNK = 256
RW_CHUNK = 64
RW_BLOCK = 1024
HALO = 8


def _params(*sem):
    return pltpu.CompilerParams(dimension_semantics=sem,
                                vmem_limit_bytes=V7X_VMEM_LIMIT_BYTES)


def _dot(a, b):
    return jnp.dot(a.astype(BF16), b.astype(BF16), preferred_element_type=F32)


def _dot_nt(a, b):
    return lax.dot_general(a.astype(BF16), b.astype(BF16), (((1,), (1,)), ((), ())),
                           preferred_element_type=F32)


def _dot_tn(a, b):
    return lax.dot_general(a.astype(BF16), b.astype(BF16), (((0,), (0,)), ((), ())),
                           preferred_element_type=F32)


def _split(x):
    hi = x.astype(BF16)
    lo = (x - hi.astype(F32)).astype(BF16)
    return hi, lo


def _dot_x_exact(x, m):
    hi, lo = _split(x)
    return (jnp.dot(hi, m, preferred_element_type=F32)
            + jnp.dot(lo, m, preferred_element_type=F32))


def _dot_exact_x(m, x):
    hi, lo = _split(x)
    return (jnp.dot(m, hi, preferred_element_type=F32)
            + jnp.dot(m, lo, preferred_element_type=F32))


def _dot3(a, b):
    ah, al = _split(a)
    bh, bl = _split(b)
    return (jnp.dot(ah, bh, preferred_element_type=F32)
            + jnp.dot(ah, bl, preferred_element_type=F32)
            + jnp.dot(al, bh, preferred_element_type=F32))


def _rms(x, g):
    ms = jnp.mean(x * x, axis=-1, keepdims=True)
    return x * lax.rsqrt(ms + NORM_EPS) * g


def _softplus(z):
    return jnp.maximum(z, 0.0) + jnp.log(1.0 + jnp.exp(-jnp.abs(z)))


def _sigmoid(z):
    return 1.0 / (1.0 + jnp.exp(-z))


def _inproj_kernel(x_ref, g_ref, w_ref, wif_ref, *out_refs, widths):
    u = _rms(x_ref[...], g_ref[...]).astype(BF16)
    c0 = 0
    for n, (o_ref, w) in enumerate(zip(out_refs[:-1], widths)):
        t = jnp.dot(u, w_ref[:, c0:c0 + w], preferred_element_type=F32)
        if n == 0:
            t = t * SB_Q_SCALE
        o_ref[...] = t.astype(o_ref.dtype)
        c0 += w
    out_refs[-1][...] = jnp.dot(u, wif_ref[...], preferred_element_type=F32)


def _even_inproj(h, g, w_main, w_if):
    T, D = h.shape
    widths = (SB_WIDTH, SB_WIDTH, SB_WIDTH, 2 * ML_WIDTH, ML_WIDTH, ML_WIDTH)
    tm = ROW_TILE
    out_shape = [jax.ShapeDtypeStruct((T, w), BF16) for w in widths]
    out_shape.append(jax.ShapeDtypeStruct((T, LANES), F32))
    out_specs = [pl.BlockSpec((tm, w), lambda i: (i, 0)) for w in widths]
    out_specs.append(pl.BlockSpec((tm, LANES), lambda i: (i, 0)))
    return pl.pallas_call(
        functools.partial(_inproj_kernel, widths=widths),
        grid=(T // tm,),
        in_specs=[pl.BlockSpec((tm, D), lambda i: (i, 0)),
                  pl.BlockSpec((1, D), lambda i: (0, 0)),
                  pl.BlockSpec(w_main.shape, lambda i: (0, 0)),
                  pl.BlockSpec(w_if.shape, lambda i: (0, 0))],
        out_specs=out_specs,
        out_shape=out_shape,
        compiler_params=_params("parallel"),
        name="even_inproj",
    )(h, g, w_main, w_if)


def _sb_kernel(q_ref, k_ref, v_ref, o_ref, *, tq, tk):
    qi = pl.program_id(2)
    q2 = q_ref[0]
    lane = lax.broadcasted_iota(jnp.int32, (1, LANES), 1)
    first_head = lane < SB_HEAD_DIM
    rr = lax.broadcasted_iota(jnp.int32, (tk, tk), 0)
    cc = lax.broadcasted_iota(jnp.int32, (tk, tk), 1)
    cum_mat = jnp.where(rr > cc, 1.0, 0.0).astype(BF16)
    row_t = lax.broadcasted_iota(jnp.int32, (tq, tk), 0)
    col_s = lax.broadcasted_iota(jnp.int32, (tq, tk), 1)
    nsub = tq // tk
    heads = (first_head, jnp.logical_not(first_head))
    qhs = [jnp.where(keep, q2, jnp.zeros_like(q2)) for keep in heads]

    def weights(qh, ks, carry, diagonal):
        zn = lax.dot_general(qh, ks, (((1,), (1,)), ((), ())), preferred_element_type=F32)
        ps = [None] * nsub
        for sub in range(nsub - 1, -1, -1):
            zs = zn[:, sub * tk:(sub + 1) * tk]
            lk = jnp.minimum(zs, 0.0) - jnp.log2(1.0 + jnp.exp2(-jnp.abs(zs)))
            if diagonal:
                causal = (col_s + sub * tk) < row_t
                lk = jnp.where(causal, lk, 0.0)
            lkb = lk.astype(BF16)
            cs = jnp.dot(lkb, cum_mat, preferred_element_type=F32)
            p = jnp.exp2(lk - zs + cs + carry)
            if diagonal:
                p = jnp.where(causal, p, 0.0)
            ps[sub] = p.astype(BF16)
            carry = carry + (cs[:, 0:1] + lkb[:, 0:1].astype(F32))
        return jnp.concatenate(ps, axis=1), carry

    def span(start, state, diagonal):
        c0, c1, acc = state
        ks = k_ref[0, pl.ds(start, tq), :]
        vs = v_ref[0, pl.ds(start, tq), :]
        p0, c0 = weights(qhs[0], ks, c0, diagonal)
        p1, c1 = weights(qhs[1], ks, c1, diagonal)
        v01 = jnp.concatenate([jnp.where(keep, vs, jnp.zeros_like(vs)) for keep in heads], axis=0)
        acc = acc + jnp.dot(jnp.concatenate([p0, p1], axis=1), v01, preferred_element_type=F32)
        return c0, c1, acc

    col0 = jnp.zeros((tq, 1), F32)
    state = span(pl.multiple_of(qi * tq, tq), (col0, col0, jnp.zeros((tq, LANES), F32)), True)

    def body(j, st):
        return span(pl.multiple_of((qi - 1 - j) * tq, tq), st, False)

    state = lax.fori_loop(0, qi, body, state)
    o_ref[0] = state[2].astype(o_ref.dtype)


def _sb_attention(q, k, v):
    B, S, W = q.shape
    tq, tk = SB_Q_TILE, SB_K_TILE
    npair = W // LANES
    return pl.pallas_call(
        functools.partial(_sb_kernel, tq=tq, tk=tk),
        grid=(B, npair, S // tq),
        in_specs=[pl.BlockSpec((1, tq, LANES), lambda b, p, i: (b, i, p)),
                  pl.BlockSpec((1, S, LANES), lambda b, p, i: (b, 0, p)),
                  pl.BlockSpec((1, S, LANES), lambda b, p, i: (b, 0, p))],
        out_specs=pl.BlockSpec((1, tq, LANES), lambda b, p, i: (b, i, p)),
        out_shape=jax.ShapeDtypeStruct((B, S, W), BF16),
        compiler_params=_params("parallel", "parallel", "arbitrary"),
        name="sb_attention",
    )(q, k, v)


def _mlstm_kernel(q_ref, k_ref, v_ref, o_ref, gn_ref, gt_ref, bn_ref, bt_ref,
                  cwq_ref, cwk_ref, hg_ref, out_ref,
                  c_ref, m_ref, pq_ref, pk_ref, *, L):
    head = pl.program_id(1)
    chunk = pl.program_id(2)

    @pl.when(chunk == 0)
    def _():
        c_ref[...] = jnp.zeros_like(c_ref)
        m_ref[...] = jnp.zeros_like(m_ref)
        pq_ref[...] = jnp.zeros_like(pq_ref)
        pk_ref[...] = jnp.zeros_like(pk_ref)

    def conv_silu(x, prev_ref, w_ref):
        xf = jnp.concatenate([prev_ref[...], x], axis=0)
        w = w_ref[0]
        y = xf * w[CONV_WIDTH - 1:CONV_WIDTH, :]
        for j in range(1, CONV_WIDTH):
            y = y + pltpu.roll(xf, j, 0) * w[CONV_WIDTH - 1 - j:CONV_WIDTH - j, :]
        prev_ref[...] = x[L - HALO:, :]
        y = y[HALO:, :]
        return y * _sigmoid(y)

    q = conv_silu(q_ref[0].astype(F32), pq_ref, cwq_ref)
    k = conv_silu(k_ref[0].astype(F32), pk_ref, cwk_ref) * (ML_HEAD_DIM ** -0.5)
    v = v_ref[0]

    lane = lax.broadcasted_iota(jnp.int32, (1, LANES), 1)
    gn = gn_ref[0] + bn_ref[...]
    li_col = jnp.sum(jnp.where(lane == head, gn, 0.0), axis=1, keepdims=True)
    f_col = jnp.sum(jnp.where(lane == head + ML_HEADS, gn, 0.0), axis=1, keepdims=True)
    li_row = gt_ref[0, pl.ds(head, 1), :] + bt_ref[pl.ds(head, 1), :]
    f_row = gt_ref[0, pl.ds(head + ML_HEADS, 1), :] + bt_ref[pl.ds(head + ML_HEADS, 1), :]
    lf_col = -_softplus(-f_col)
    lf_row = -_softplus(-f_row)

    ri = lax.broadcasted_iota(jnp.int32, (L, L), 0)
    ci = lax.broadcasted_iota(jnp.int32, (L, L), 1)
    tril = ci <= ri
    tril_m = jnp.where(tril, 1.0, 0.0).astype(BF16)
    triu_m = jnp.where(ci >= ri, 1.0, 0.0).astype(BF16)
    b_col = _dot_exact_x(tril_m, jnp.broadcast_to(lf_col, (L, LANES)))[:, :1]
    b_row = _dot_x_exact(jnp.broadcast_to(lf_row, (HALO, L)), triu_m)[:1, :]

    m_prev = m_ref[0:1, 0:1]
    dmat = jnp.where(tril, b_col - b_row + li_row, -jnp.inf)
    inter = b_col + m_prev
    m_t = jnp.maximum(jnp.max(dmat, axis=1, keepdims=True), inter)
    scores = _dot_nt(q, k) * jnp.exp(dmat - m_t)
    w_inter = jnp.exp(inter - m_t)
    v_aug = jnp.concatenate([v, jnp.ones_like(v)], axis=1)
    c_aug = c_ref[...]
    num_aug = _dot(scores, v_aug) + w_inter * _dot(q, c_aug)
    num = num_aug[:, :LANES]
    den = num_aug[:, LANES:LANES + 1]
    hval = num / jnp.maximum(jnp.abs(den), jnp.exp(-m_t))

    b_last = b_col[L - 1:L, :]
    gcol = b_last - b_col + li_col
    m_new = jnp.maximum(b_last + m_prev, jnp.max(gcol, axis=0, keepdims=True))
    w_state = jnp.exp(b_last + m_prev - m_new)
    w_tok = jnp.exp(gcol - m_new)
    c_ref[...] = w_state * c_aug + _dot_tn(k * w_tok, v_aug)
    m_ref[...] = jnp.broadcast_to(m_new, m_ref.shape)

    hn = hval * lax.rsqrt(jnp.mean(hval * hval, axis=-1, keepdims=True) + NORM_EPS) * hg_ref[0]
    out_ref[0] = (hn * _sigmoid(o_ref[0].astype(F32))).astype(out_ref.dtype)


def _mlstm(qk, v, o, gates_nat, gates_t, bias_nat, bias_t, conv_w, head_g):
    B, S, _ = v.shape
    L = ML_CHUNK
    blk = lambda off: pl.BlockSpec((1, L, LANES), lambda b, h, c: (b, c, h + off))
    return pl.pallas_call(
        functools.partial(_mlstm_kernel, L=L),
        grid=(B, ML_HEADS, S // L),
        in_specs=[blk(0), blk(ML_HEADS), blk(0), blk(0),
                  pl.BlockSpec((1, L, LANES), lambda b, h, c: (b, c, 0)),
                  pl.BlockSpec((1, 2 * ML_HEADS, L), lambda b, h, c: (b, 0, c)),
                  pl.BlockSpec((1, LANES), lambda b, h, c: (0, 0)),
                  pl.BlockSpec((2 * ML_HEADS, L), lambda b, h, c: (0, 0)),
                  pl.BlockSpec((1, CONV_WIDTH, LANES), lambda b, h, c: (h, 0, 0)),
                  pl.BlockSpec((1, CONV_WIDTH, LANES), lambda b, h, c: (h + ML_HEADS, 0, 0)),
                  pl.BlockSpec((1, 1, LANES), lambda b, h, c: (h, 0, 0))],
        out_specs=pl.BlockSpec((1, L, LANES), lambda b, h, c: (b, c, h)),
        out_shape=jax.ShapeDtypeStruct((B, S, ML_WIDTH), BF16),
        scratch_shapes=[pltpu.VMEM((ML_HEAD_DIM, 2 * LANES), F32),
                        pltpu.VMEM((HALO, LANES), F32),
                        pltpu.VMEM((HALO, LANES), F32),
                        pltpu.VMEM((HALO, LANES), F32)],
        compiler_params=_params("parallel", "parallel", "arbitrary"),
        name="mlstm",
    )(qk, qk, v, o, gates_nat, gates_t, bias_nat, bias_t, conv_w, conv_w, head_g)


def _mixout_kernel(*refs, nparts):
    parts = refs[:nparts]
    w_ref, h_ref, g_ref, o_ref = refs[nparts:]
    c0 = 0
    mix = None
    for p in parts:
        w = p.shape[1]
        t = jnp.dot(p[...], w_ref[c0:c0 + w, :], preferred_element_type=F32)
        mix = t if mix is None else mix + t
        c0 += w
    o_ref[...] = h_ref[...] + _rms(mix, g_ref[...])


def _mix_out(parts, w_out, h, g):
    T, D = h.shape
    tm = ROW_TILE
    in_specs = [pl.BlockSpec((tm, p.shape[1]), lambda i: (i, 0)) for p in parts]
    in_specs += [pl.BlockSpec(w_out.shape, lambda i: (0, 0)),
                 pl.BlockSpec((tm, D), lambda i: (i, 0)),
                 pl.BlockSpec((1, D), lambda i: (0, 0))]
    return pl.pallas_call(
        functools.partial(_mixout_kernel, nparts=len(parts)),
        grid=(T // tm,),
        in_specs=in_specs,
        out_specs=pl.BlockSpec((tm, D), lambda i: (i, 0)),
        out_shape=jax.ShapeDtypeStruct((T, D), F32),
        compiler_params=_params("parallel"),
        name="mix_out",
    )(*parts, w_out, h, g)


def _mlp_kernel(h_ref, g2_ref, wu_ref, wd_ref, g3_ref, o_ref, u_ref, acc_ref):
    j = pl.program_id(1)

    @pl.when(j == 0)
    def _():
        u_ref[...] = _rms(h_ref[...], g2_ref[...]).astype(BF16)
        acc_ref[...] = jnp.zeros_like(acc_ref)

    a = jnp.dot(u_ref[...], wu_ref[...], preferred_element_type=F32)
    a = jnp.maximum(a, 0.0)
    acc_ref[...] += jnp.dot((a * a).astype(BF16), wd_ref[...], preferred_element_type=F32)

    @pl.when(j == pl.num_programs(1) - 1)
    def _():
        o_ref[...] = h_ref[...] + _rms(acc_ref[...], g3_ref[...])


def _mlp(h, g2, w_up, w_down, g3):
    T, D = h.shape
    FF = w_up.shape[1]
    tm, tf = MLP_ROW_TILE, MLP_FF_TILE
    return pl.pallas_call(
        _mlp_kernel,
        grid=(T // tm, FF // tf),
        in_specs=[pl.BlockSpec((tm, D), lambda i, j: (i, 0)),
                  pl.BlockSpec((1, D), lambda i, j: (0, 0)),
                  pl.BlockSpec((D, tf), lambda i, j: (0, j)),
                  pl.BlockSpec((tf, D), lambda i, j: (j, 0)),
                  pl.BlockSpec((1, D), lambda i, j: (0, 0))],
        out_specs=pl.BlockSpec((tm, D), lambda i, j: (i, 0)),
        out_shape=jax.ShapeDtypeStruct((T, D), F32),
        scratch_shapes=[pltpu.VMEM((tm, D), BF16), pltpu.VMEM((tm, D), F32)],
        compiler_params=_params("parallel", "arbitrary"),
        name="mlp",
    )(h, g2, w_up, w_down, g3)


def _rwproj_kernel(x_ref, xh_ref, g_ref, mu_ref, wr_ref, wk_ref, wv_ref,
                   w1_ref, w2_ref, a1_ref, a2_ref, g1_ref, g2_ref, vec_ref,
                   r_ref, lw_ref, k_ref, v_ref, kk_ref, a_ref, gate_ref, *, tm, seq):
    i = pl.program_id(0)
    g = g_ref[...]
    u = _rms(x_ref[...], g)
    not_first = jnp.where((i * tm) % seq == 0, 0.0, 1.0)
    up = _rms(xh_ref[HALO - 1:HALO, :], g) * not_first
    row = lax.broadcasted_iota(jnp.int32, (tm, 1), 0)
    x_prev = jnp.where(row == 0, up, pltpu.roll(u, 1, 0))
    xx = x_prev - u
    mix = lambda n: (u + xx * mu_ref[n:n + 1, :]).astype(BF16)
    w0, a0, k_k, k_a = (vec_ref[n:n + 1, :] for n in range(4))

    r = jnp.dot(mix(0), wr_ref[...], preferred_element_type=F32)
    k = jnp.dot(mix(2), wk_ref[...], preferred_element_type=F32)
    v = jnp.dot(mix(3), wv_ref[...], preferred_element_type=F32)
    dw = _dot(jnp.tanh(jnp.dot(mix(1), w1_ref[...], preferred_element_type=F32)), w2_ref[...])
    log_w = -_softplus(-(w0 + dw)) - 0.5
    a = _sigmoid(a0 + _dot(jnp.dot(mix(4), a1_ref[...], preferred_element_type=F32), a2_ref[...]))
    gate = _dot(_sigmoid(jnp.dot(mix(5), g1_ref[...], preferred_element_type=F32)), g2_ref[...])

    r_ref[...] = r
    lw_ref[...] = -jnp.exp(log_w)
    k_ref[...] = k * (1.0 + (a - 1.0) * k_a)
    v_ref[...] = v
    kk_ref[...] = k * k_k
    a_ref[...] = a
    gate_ref[...] = gate


def _rwkv_proj(h, g, mu, wr, wk, wv, w1, w2, a1, a2, g1, g2, vecs, seq):
    T, D = h.shape
    tm = RW_ROW_TILE
    full = lambda arr: pl.BlockSpec(arr.shape, lambda i: (0, 0))
    row = pl.BlockSpec((tm, D), lambda i: (i, 0))
    halo = pl.BlockSpec((HALO, D), lambda i: (jnp.maximum(i * (tm // HALO) - 1, 0), 0))
    return pl.pallas_call(
        functools.partial(_rwproj_kernel, tm=tm, seq=seq),
        grid=(T // tm,),
        in_specs=[row, halo, full(g), full(mu), full(wr), full(wk), full(wv),
                  full(w1), full(w2), full(a1), full(a2), full(g1), full(g2), full(vecs)],
        out_specs=[row] * 7,
        out_shape=[jax.ShapeDtypeStruct((T, D), F32)] * 7,
        compiler_params=_params("parallel"),
        name="rwkv_proj",
    )(h, h, g, mu, wr, wk, wv, w1, w2, a1, a2, g1, g2, vecs)


def _bmm(a, b):
    return lax.dot_general(a.astype(BF16), b.astype(BF16), (((2,), (1,)), ((0,), (0,))),
                           preferred_element_type=F32)


def _bmm_nt(a, b):
    return lax.dot_general(a.astype(BF16), b.astype(BF16), (((2,), (2,)), ((0,), (0,))),
                           preferred_element_type=F32)


def _bmm_tn(a, b):
    return lax.dot_general(a.astype(BF16), b.astype(BF16), (((1,), (1,)), ((0,), (0,))),
                           preferred_element_type=F32)


def _rwscan_kernel(r_ref, lw_ref, k_ref, v_ref, kk_ref, a_ref, gate_ref,
                   rk_ref, lng_ref, lnb_ref, o_ref, s_ref, *, L, nsub):
    @pl.when(pl.program_id(2) == 0)
    def _():
        s_ref[...] = jnp.zeros_like(s_ref)

    P = 2 * L
    lane = lax.broadcasted_iota(jnp.int32, (1, 1, LANES), 2)
    m0 = jnp.where(lane < RW_HEAD_DIM, 1.0, 0.0)
    m1 = 1.0 - m0
    ri = lax.broadcasted_iota(jnp.int32, (P, P), 0)
    ci = lax.broadcasted_iota(jnp.int32, (P, P), 1)
    same = (ri >> 6) == (ci >> 6)
    gsum = jnp.where(same, 1.0, 0.0).astype(BF16)
    strict = same & (ci < ri)
    incl = same & (ci <= ri)
    ri_l = lax.broadcasted_iota(jnp.int32, (L, L), 0)
    ci_l = lax.broadcasted_iota(jnp.int32, (L, L), 1)
    tril_l = jnp.where(ci_l <= ri_l, 1.0, 0.0).astype(BF16)

    blk = lambda ref: ref[0].reshape(nsub, L, LANES)
    r, lw, k, v, kk, a = (blk(x) for x in (r_ref, lw_ref, k_ref, v_ref, kk_ref, a_ref))

    def lane_group_sum(x):
        return _dot_x_exact(x, gsum)

    ssq = lane_group_sum((kk * kk).reshape(nsub * L, LANES)).reshape(nsub, L, LANES)
    kap = kk * lax.rsqrt(jnp.maximum(ssq, 1e-24))
    bv = kap * a

    t1 = lw.astype(BF16)
    r1 = lw - t1.astype(F32)
    t2 = r1.astype(BF16)
    t3 = (r1 - t2.astype(F32)).astype(BF16)
    tril_b = jnp.broadcast_to(tril_l, (nsub, L, L))
    cum = (_bmm(tril_b, t1) + _bmm(tril_b, t2)) + _bmm(tril_b, t3)
    c_last = cum[:, L - 1:L, :]
    w_incl = jnp.exp(cum)
    w_excl = jnp.exp(cum - lw)
    w_inv = jnp.exp(-cum)
    w_rem = jnp.exp(c_last - cum)
    w_last = jnp.exp(c_last)

    stack = lambda x: jnp.concatenate([x * m0, x * m1], axis=1)
    dup = lambda x: jnp.concatenate([x, x], axis=1)
    unstack = lambda x: x[:, :L, :] + x[:, L:, :]

    a_st = stack(-kap * w_excl)
    r_st = stack(r * w_incl)
    v_st = stack(v)
    gram = _bmm_nt(jnp.concatenate([a_st, r_st], axis=1),
                   jnp.concatenate([dup(bv * w_inv), dup(k * w_inv)], axis=1))
    a_ab = jnp.where(strict, gram[:, :P, :P], 0.0)
    a_ak = jnp.where(strict, gram[:, :P, P:], 0.0)
    a_rb = jnp.where(incl, gram[:, P:, :P], 0.0)
    a_rk = jnp.where(incl, gram[:, P:, P:], 0.0)

    tinv = jnp.where(ri == ci, 1.0, 0.0) + jnp.where((ri >> 1) == (ci >> 1), a_ab, 0.0)
    n = 2
    while n < L:
        sh = n.bit_length() - 1
        off = ((ri >> (sh + 1)) == (ci >> (sh + 1))) & ((ri >> sh) != (ci >> sh))
        e = jnp.where(off, a_ab, 0.0)
        tinv = tinv + _bmm(tinv, _bmm(e, tinv))
        n *= 2

    x1 = _bmm(a_ak, v_st)
    tu = _bmm(tinv, jnp.concatenate([x1, a_st], axis=2))
    u0_st, ta_st = tu[:, :, :LANES], tu[:, :, LANES:]
    yr = _bmm(a_rb, tu)
    y0 = unstack(yr[:, :, :LANES] + _bmm(a_rk, v_st))
    rq = unstack(r_st + yr[:, :, LANES:])
    bh_st = stack(bv * w_rem)
    kh_st = stack(k * w_rem)
    mn = _bmm_tn(bh_st, jnp.concatenate([ta_st, u0_st], axis=2))
    m_bd = mn[:, :, :LANES] + jnp.where(ri == ci, w_last, 0.0)
    n_bd = mn[:, :, LANES:] + _bmm_tn(kh_st, v_st)

    s = s_ref[...]
    ys = []
    for c in range(nsub):
        ys.append(_dot(rq[c], s) + y0[c])
        s = _dot3(m_bd[c], s) + n_bd[c]
    s_ref[...] = s
    y = jnp.concatenate(ys, axis=0)

    inv_d = 1.0 / RW_HEAD_DIM
    mean = lane_group_sum(y) * inv_d
    d = y - mean
    var = lane_group_sum(d * d) * inv_d
    yn = d * lax.rsqrt(var + GN_EPS) * lng_ref[...] + lnb_ref[...]
    r2, k2, v2 = r_ref[0], k_ref[0], v_ref[0]
    bonus = lane_group_sum(r2 * k2 * rk_ref[...]) * v2
    o_ref[0] = ((yn + bonus) * gate_ref[0]).astype(o_ref.dtype)


def _rwkv_scan(r, lw, k, v, kk, a, gate, r_k, ln_g, ln_b):
    B, S, D = r.shape
    L, lb = RW_CHUNK, RW_BLOCK
    blk = pl.BlockSpec((1, lb, LANES), lambda b, p, j: (b, j, p))
    vec = pl.BlockSpec((1, LANES), lambda b, p, j: (0, p))
    return pl.pallas_call(
        functools.partial(_rwscan_kernel, L=L, nsub=lb // L),
        grid=(B, D // LANES, S // lb),
        in_specs=[blk] * 7 + [vec] * 3,
        out_specs=blk,
        out_shape=jax.ShapeDtypeStruct((B, S, D), BF16),
        scratch_shapes=[pltpu.VMEM((LANES, LANES), F32)],
        compiler_params=_params("parallel", "parallel", "arbitrary"),
        name="rwkv_scan",
    )(r, lw, k, v, kk, a, gate, r_k, ln_g, ln_b)


def _even_layer(h, B, S, g, w_in, b_if, conv_w, head_g, w_out):
    T, D = h.shape
    ncols = 3 * SB_WIDTH + 4 * ML_WIDTH
    w_main = w_in[:, :ncols].astype(BF16)
    w_if = jnp.pad(w_in[:, ncols:], ((0, 0), (0, LANES - 2 * ML_HEADS))).astype(BF16)
    sb_q, sb_k, sb_v, ml_qk, ml_v, ml_o, ml_if = _even_inproj(h, g[0:1], w_main, w_if)
    to3 = lambda t: t.reshape(B, S, t.shape[-1])
    a_out = _sb_attention(to3(sb_q), to3(sb_k), to3(sb_v))
    gates_nat = to3(ml_if)
    gates_t = jnp.swapaxes(gates_nat[:, :, :2 * ML_HEADS], 1, 2)
    bias_nat = jnp.pad(b_if, (0, LANES - 2 * ML_HEADS)).reshape(1, LANES)
    bias_t = jnp.broadcast_to(b_if[:, None], (2 * ML_HEADS, ML_CHUNK))
    cw = conv_w.reshape(CONV_WIDTH, 2 * ML_HEADS, LANES).transpose(1, 0, 2)
    hm = _mlstm(to3(ml_qk), to3(ml_v), to3(ml_o), gates_nat, gates_t, bias_nat, bias_t,
                cw, head_g.reshape(ML_HEADS, 1, LANES))
    return _mix_out([a_out.reshape(T, SB_WIDTH), hm.reshape(T, ML_WIDTH)],
                    w_out.astype(BF16), h, g[1:2])


def _odd_layer(h, B, S, g, mu, w_rkv, w0, w1, w2, a0, a1, a2, g1, g2, k_k, k_a, r_k,
               ln_g, ln_b, w_out):
    T, D = h.shape
    bf = lambda t: t.astype(BF16)
    mu8 = jnp.pad(mu, ((0, HALO - mu.shape[0]), (0, 0)))
    vecs = jnp.pad(jnp.stack([w0, a0, k_k, k_a]), ((0, HALO - 4), (0, 0)))
    r, lw, k, v, kk, a, gate = _rwkv_proj(
        h, g[0:1], mu8, bf(w_rkv[0]), bf(w_rkv[1]), bf(w_rkv[2]),
        bf(w1), bf(w2), bf(a1), bf(a2), bf(g1), bf(g2), vecs, S)
    to3 = lambda t: t.reshape(B, S, D)
    y = _rwkv_scan(to3(r), to3(lw), to3(k), to3(v), to3(kk), to3(a), to3(gate),
                   r_k.reshape(1, D), ln_g.reshape(1, D), ln_b.reshape(1, D))
    return _mix_out([y.reshape(T, D)], bf(w_out), h, g[1:2])


def kernel(x, norm_g, e_w_in, e_b_if, e_conv_w, e_head_g, e_w_out, r_mu, r_w_rkv, r_w0, r_w1, r_w2, r_a0, r_a1, r_a2, r_g1, r_g2, r_k_k, r_k_a, r_r_k, r_ln_g, r_ln_b, r_w_out, mlp_w_up, mlp_w_down):
    B, S, D = x.shape
    h = x.reshape(B * S, D)
    for layer in range(norm_g.shape[0]):
        g = norm_g[layer]
        if layer % 2 == 0:
            e = layer // 2
            h = _even_layer(h, B, S, g, e_w_in[e], e_b_if[e], e_conv_w[e], e_head_g[e],
                            e_w_out[e])
        else:
            o = layer // 2
            h = _odd_layer(h, B, S, g, r_mu[o], r_w_rkv[o], r_w0[o], r_w1[o], r_w2[o],
                           r_a0[o], r_a1[o], r_a2[o], r_g1[o], r_g2[o], r_k_k[o], r_k_a[o],
                           r_r_k[o], r_ln_g[o], r_ln_b[o], r_w_out[o])
        h = _mlp(h, g[2:3], mlp_w_up[layer].astype(BF16), mlp_w_down[layer].astype(BF16),
                 g[3:4])
    return h.reshape(B, S, D)
```

```python
import functools
import math

import jax
import jax.numpy as jnp
from jax import lax
from jax.experimental import pallas as pl
from jax.experimental.pallas import tpu as pltpu

F32 = jnp.float32
BF16 = jnp.bfloat16

LANES = 128
V7X_VMEM_LIMIT_BYTES = 56 * 1024 * 1024

SB_HEADS = 8
SB_HEAD_DIM = 64
SB_WIDTH = SB_HEADS * SB_HEAD_DIM
ML_HEADS = 4
ML_HEAD_DIM = 128
ML_WIDTH = ML_HEADS * ML_HEAD_DIM
CONV_WIDTH = 4
RW_HEAD_DIM = 64
SB_Q_SCALE = -math.log2(math.e) / math.sqrt(SB_HEAD_DIM)
NORM_EPS = 1e-6
GN_EPS = 64e-5

ROW_TILE = 512
RW_ROW_TILE = 512
MLP_ROW_TILE = 1024
MLP_FF_TILE = 1024
SB_Q_TILE = 512
SB_K_TILE = 256
ML_CHUNK = 256
RW_CHUNK = 64
RW_BLOCK = 256
RW_PAIRS = 4
HALO = 8


def _params(*sem):
    return pltpu.CompilerParams(dimension_semantics=sem,
                                vmem_limit_bytes=V7X_VMEM_LIMIT_BYTES)


def _dot(a, b):
    return jnp.dot(a.astype(BF16), b.astype(BF16), preferred_element_type=F32)


def _dot_nt(a, b):
    return lax.dot_general(a.astype(BF16), b.astype(BF16), (((1,), (1,)), ((), ())),
                           preferred_element_type=F32)


def _dot_tn(a, b):
    return lax.dot_general(a.astype(BF16), b.astype(BF16), (((0,), (0,)), ((), ())),
                           preferred_element_type=F32)


def _split(x):
    hi = x.astype(BF16)
    lo = (x - hi.astype(F32)).astype(BF16)
    return hi, lo


def _dot_x_exact(x, m):
    hi, lo = _split(x)
    return (jnp.dot(hi, m, preferred_element_type=F32)
            + jnp.dot(lo, m, preferred_element_type=F32))


def _dot_exact_x(m, x):
    hi, lo = _split(x)
    return (jnp.dot(m, hi, preferred_element_type=F32)
            + jnp.dot(m, lo, preferred_element_type=F32))


def _dot3(a, b):
    ah, al = _split(a)
    bh, bl = _split(b)
    return (jnp.dot(ah, bh, preferred_element_type=F32)
            + jnp.dot(ah, bl, preferred_element_type=F32)
            + jnp.dot(al, bh, preferred_element_type=F32))


def _rms(x, g):
    ms = jnp.mean(x * x, axis=-1, keepdims=True)
    return x * lax.rsqrt(ms + NORM_EPS) * g


def _softplus(z):
    return jnp.maximum(z, 0.0) + jnp.log(1.0 + jnp.exp(-jnp.abs(z)))


def _neg_abs(x):
    bits = lax.bitcast_convert_type(x, jnp.uint32) | jnp.uint32(0x80000000)
    return lax.bitcast_convert_type(bits, F32)


def _sigmoid(z):
    return 1.0 / (1.0 + jnp.exp(-z))


def _inproj_kernel(x_ref, g_ref, w_ref, wif_ref, *out_refs, widths):
    u = _rms(x_ref[...], g_ref[...]).astype(BF16)
    c0 = 0
    for n, (o_ref, w) in enumerate(zip(out_refs[:-1], widths)):
        t = jnp.dot(u, w_ref[:, c0:c0 + w], preferred_element_type=F32)
        if n == 0:
            t = t * SB_Q_SCALE
        o_ref[...] = t.astype(o_ref.dtype)
        c0 += w
    out_refs[-1][...] = jnp.dot(u, wif_ref[...], preferred_element_type=F32)


def _even_inproj(h, g, w_main, w_if):
    T, D = h.shape
    widths = (SB_WIDTH, SB_WIDTH, SB_WIDTH, 2 * ML_WIDTH, ML_WIDTH, ML_WIDTH)
    tm = ROW_TILE
    out_shape = [jax.ShapeDtypeStruct((T, w), BF16) for w in widths]
    out_shape.append(jax.ShapeDtypeStruct((T, LANES), F32))
    out_specs = [pl.BlockSpec((tm, w), lambda i: (i, 0)) for w in widths]
    out_specs.append(pl.BlockSpec((tm, LANES), lambda i: (i, 0)))
    return pl.pallas_call(
        functools.partial(_inproj_kernel, widths=widths),
        grid=(T // tm,),
        in_specs=[pl.BlockSpec((tm, D), lambda i: (i, 0)),
                  pl.BlockSpec((1, D), lambda i: (0, 0)),
                  pl.BlockSpec(w_main.shape, lambda i: (0, 0)),
                  pl.BlockSpec(w_if.shape, lambda i: (0, 0))],
        out_specs=out_specs,
        out_shape=out_shape,
        compiler_params=_params("parallel"),
        name="even_inproj",
    )(h, g, w_main, w_if)


def _sb_kernel(q_ref, k_ref, v_ref, o_ref, *, tq, tk):
    qi = pl.program_id(2)
    q2 = q_ref[0]
    lane = lax.broadcasted_iota(jnp.int32, (1, LANES), 1)
    first_head = lane < SB_HEAD_DIM
    rr = lax.broadcasted_iota(jnp.int32, (tk, tk), 0)
    cc = lax.broadcasted_iota(jnp.int32, (tk, tk), 1)
    cum_mat = jnp.where(rr > cc, 1.0, 0.0).astype(BF16)
    row_t = lax.broadcasted_iota(jnp.int32, (tq, tk), 0)
    col_s = lax.broadcasted_iota(jnp.int32, (tq, tk), 1)
    nsub = tq // tk
    heads = (first_head, jnp.logical_not(first_head))
    qhs = [jnp.where(keep, q2, jnp.zeros_like(q2)) for keep in heads]

    def weights(qh, ks, carry, diagonal):
        zn = lax.dot_general(qh, ks, (((1,), (1,)), ((), ())), preferred_element_type=F32)
        ps = [None] * nsub
        for sub in range(nsub - 1, -1, -1):
            zs = zn[:, sub * tk:(sub + 1) * tk]
            lk = jnp.minimum(zs, 0.0) - jnp.log2(1.0 + jnp.exp2(_neg_abs(zs)))
            if diagonal:
                causal = (col_s + sub * tk) < row_t
                lk = jnp.where(causal, lk, 0.0)
            lkb = lk.astype(BF16)
            cs = jnp.dot(lkb, cum_mat, preferred_element_type=F32)
            p = jnp.exp2(lk - zs + cs + carry)
            if diagonal:
                p = jnp.where(causal, p, 0.0)
            ps[sub] = p.astype(BF16)
            carry = carry + (cs[:, 0:1] + lkb[:, 0:1].astype(F32))
        return jnp.concatenate(ps, axis=1), carry

    def span(start, state, diagonal):
        c0, c1, acc = state
        ks = k_ref[0, pl.ds(start, tq), :]
        vs = v_ref[0, pl.ds(start, tq), :]
        p0, c0 = weights(qhs[0], ks, c0, diagonal)
        p1, c1 = weights(qhs[1], ks, c1, diagonal)
        v01 = jnp.concatenate([jnp.where(keep, vs, jnp.zeros_like(vs)) for keep in heads], axis=0)
        acc = acc + jnp.dot(jnp.concatenate([p0, p1], axis=1), v01, preferred_element_type=F32)
        return c0, c1, acc

    col0 = jnp.zeros((tq, 1), F32)
    state = span(pl.multiple_of(qi * tq, tq), (col0, col0, jnp.zeros((tq, LANES), F32)), True)

    def body(j, st):
        return span(pl.multiple_of((qi - 1 - j) * tq, tq), st, False)

    state = lax.fori_loop(0, qi, body, state)
    o_ref[0] = state[2].astype(o_ref.dtype)


def _sb_attention(q, k, v):
    B, S, W = q.shape
    tq, tk = SB_Q_TILE, SB_K_TILE
    npair = W // LANES
    return pl.pallas_call(
        functools.partial(_sb_kernel, tq=tq, tk=tk),
        grid=(B, npair, S // tq),
        in_specs=[pl.BlockSpec((1, tq, LANES), lambda b, p, i: (b, i, p)),
                  pl.BlockSpec((1, S, LANES), lambda b, p, i: (b, 0, p)),
                  pl.BlockSpec((1, S, LANES), lambda b, p, i: (b, 0, p))],
        out_specs=pl.BlockSpec((1, tq, LANES), lambda b, p, i: (b, i, p)),
        out_shape=jax.ShapeDtypeStruct((B, S, W), BF16),
        compiler_params=_params("parallel", "parallel", "arbitrary"),
        name="sb_attention",
    )(q, k, v)


def _mlstm_kernel(q_ref, k_ref, v_ref, o_ref, gn_ref, gt_ref, bn_ref, bt_ref,
                  cwq_ref, cwk_ref, hg_ref, out_ref,
                  c_ref, m_ref, pq_ref, pk_ref, *, L):
    chunk = pl.program_id(1)

    @pl.when(chunk == 0)
    def _():
        c_ref[...] = jnp.zeros_like(c_ref)
        m_ref[...] = jnp.zeros_like(m_ref)
        pq_ref[...] = jnp.zeros_like(pq_ref)
        pk_ref[...] = jnp.zeros_like(pk_ref)

    def conv_silu(x, prev, w):
        xf = jnp.concatenate([prev, x], axis=0)
        y = xf * w[CONV_WIDTH - 1:CONV_WIDTH, :]
        for j in range(1, CONV_WIDTH):
            y = y + pltpu.roll(xf, j, 0) * w[CONV_WIDTH - 1 - j:CONV_WIDTH - j, :]
        y = y[HALO:, :]
        return y * _sigmoid(y)

    ri = lax.broadcasted_iota(jnp.int32, (L, L), 0)
    ci = lax.broadcasted_iota(jnp.int32, (L, L), 1)
    tril = ci <= ri
    tril_m = jnp.where(tril, 1.0, 0.0).astype(BF16)
    triu_m = jnp.where(ci >= ri, 1.0, 0.0).astype(BF16)
    gn = gn_ref[0] + bn_ref[...]
    gt = gt_ref[0] + bt_ref[...]
    ones = jnp.ones((L, LANES), BF16)

    def one_head(head):
        sl = slice(head * LANES, (head + 1) * LANES)
        q_raw = q_ref[0, :, sl].astype(F32)
        k_raw = k_ref[0, :, sl].astype(F32)
        q = conv_silu(q_raw, pq_ref[:, sl], cwq_ref[head])
        k = conv_silu(k_raw, pk_ref[:, sl], cwk_ref[head + ML_HEADS]) * (ML_HEAD_DIM ** -0.5)
        pq_ref[:, sl] = q_raw[L - HALO:, :]
        pk_ref[:, sl] = k_raw[L - HALO:, :]
        v = v_ref[0, :, sl]

        li_col = gn[:, head:head + 1]
        lf_col = -_softplus(-gn[:, head + ML_HEADS:head + ML_HEADS + 1])
        li_row = gt[head:head + 1, :]
        lf_row = -_softplus(-gt[head + ML_HEADS:head + ML_HEADS + 1, :])
        b_col = _dot_exact_x(tril_m, jnp.broadcast_to(lf_col, (L, LANES)))[:, :1]
        b_row = _dot_x_exact(jnp.broadcast_to(lf_row, (HALO, L)), triu_m)[:1, :]

        m_prev = m_ref[head, 0:1, 0:1]
        dmat = jnp.where(tril, b_col - b_row + li_row, -jnp.inf)
        inter = b_col + m_prev
        m_t = jnp.maximum(jnp.max(dmat, axis=1, keepdims=True), inter)
        scores = _dot_nt(q, k) * jnp.exp(dmat - m_t)
        w_inter = jnp.exp(inter - m_t)
        v_aug = jnp.concatenate([v, ones], axis=1)
        c_aug = c_ref[head]
        num_aug = _dot(scores, v_aug) + w_inter * _dot(q, c_aug)
        num = num_aug[:, :LANES]
        den = num_aug[:, LANES:LANES + 1]
        hval = num / jnp.maximum(jnp.abs(den), jnp.exp(-m_t))

        b_last = b_col[L - 1:L, :]
        gcol = b_last - b_col + li_col
        m_new = jnp.maximum(b_last + m_prev, jnp.max(gcol, axis=0, keepdims=True))
        w_state = jnp.exp(b_last + m_prev - m_new)
        w_tok = jnp.exp(gcol - m_new)
        c_ref[head] = w_state * c_aug + _dot_tn(k * w_tok, v_aug)
        m_ref[head] = jnp.broadcast_to(m_new, (HALO, LANES))

        hn = hval * lax.rsqrt(jnp.mean(hval * hval, axis=-1, keepdims=True) + NORM_EPS) * hg_ref[head]
        out_ref[0, :, sl] = (hn * _sigmoid(o_ref[0, :, sl].astype(F32))).astype(out_ref.dtype)

    for head in range(ML_HEADS):
        one_head(head)


def _mlstm(qk, v, o, gates_nat, gates_t, bias_nat, bias_t, conv_w, head_g):
    B, S, W = v.shape
    L = ML_CHUNK
    blk = lambda off: pl.BlockSpec((1, L, W), lambda b, c: (b, c, off))
    full = lambda arr: pl.BlockSpec(arr.shape, lambda b, c: (0,) * arr.ndim)
    return pl.pallas_call(
        functools.partial(_mlstm_kernel, L=L),
        grid=(B, S // L),
        in_specs=[blk(0), blk(1), blk(0), blk(0),
                  pl.BlockSpec((1, L, LANES), lambda b, c: (b, c, 0)),
                  pl.BlockSpec((1, 2 * ML_HEADS, L), lambda b, c: (b, 0, c)),
                  full(bias_nat), full(bias_t), full(conv_w), full(conv_w), full(head_g)],
        out_specs=pl.BlockSpec((1, L, W), lambda b, c: (b, c, 0)),
        out_shape=jax.ShapeDtypeStruct((B, S, W), BF16),
        scratch_shapes=[pltpu.VMEM((ML_HEADS, ML_HEAD_DIM, 2 * LANES), F32),
                        pltpu.VMEM((ML_HEADS, HALO, LANES), F32),
                        pltpu.VMEM((HALO, W), F32),
                        pltpu.VMEM((HALO, W), F32)],
        compiler_params=_params("parallel", "arbitrary"),
        name="mlstm",
    )(qk, qk, v, o, gates_nat, gates_t, bias_nat, bias_t, conv_w, conv_w, head_g)


def _mixmlp_kernel(*refs, nparts):
    parts = refs[:nparts]
    wo_ref, h_ref, g_ref, wu_ref, wd_ref, o_ref, u_ref, acc_ref = refs[nparts:]
    j = pl.program_id(1)

    @pl.when(j == 0)
    def _():
        c0 = 0
        mix = None
        for p in parts:
            w = p.shape[1]
            t = jnp.dot(p[...], wo_ref[c0:c0 + w, :], preferred_element_type=F32)
            mix = t if mix is None else mix + t
            c0 += w
        h1 = h_ref[...] + _rms(mix, g_ref[1:2, :])
        o_ref[...] = h1
        u_ref[...] = _rms(h1, g_ref[2:3, :]).astype(BF16)
        acc_ref[...] = jnp.zeros_like(acc_ref)

    a = jnp.dot(u_ref[...], wu_ref[...], preferred_element_type=F32)
    a = jnp.maximum(a, 0.0)
    acc_ref[...] += jnp.dot((a * a).astype(BF16), wd_ref[...], preferred_element_type=F32)

    @pl.when(j == pl.num_programs(1) - 1)
    def _():
        o_ref[...] = o_ref[...] + _rms(acc_ref[...], g_ref[3:4, :])


def _mix_mlp(parts, w_out, h, g, w_up, w_down):
    T, D = h.shape
    FF = w_up.shape[1]
    tm, tf = MLP_ROW_TILE, MLP_FF_TILE
    in_specs = [pl.BlockSpec((tm, p.shape[1]), lambda i, j: (i, 0)) for p in parts]
    in_specs += [pl.BlockSpec(w_out.shape, lambda i, j: (0, 0), pipeline_mode=pl.Buffered(1)),
                 pl.BlockSpec((tm, D), lambda i, j: (i, 0)),
                 pl.BlockSpec(g.shape, lambda i, j: (0, 0)),
                 pl.BlockSpec((D, tf), lambda i, j: (0, j)),
                 pl.BlockSpec((tf, D), lambda i, j: (j, 0))]
    return pl.pallas_call(
        functools.partial(_mixmlp_kernel, nparts=len(parts)),
        grid=(T // tm, FF // tf),
        in_specs=in_specs,
        out_specs=pl.BlockSpec((tm, D), lambda i, j: (i, 0)),
        out_shape=jax.ShapeDtypeStruct((T, D), F32),
        scratch_shapes=[pltpu.VMEM((tm, D), BF16), pltpu.VMEM((tm, D), F32)],
        compiler_params=_params("parallel", "arbitrary"),
        name="mix_mlp",
    )(*parts, w_out, h, g, w_up, w_down)


def _rwproj_kernel(x_ref, xh_ref, g_ref, mu_ref, wr_ref, wk_ref, wv_ref,
                   w1_ref, w2_ref, a1_ref, a2_ref, g1_ref, g2_ref, vec_ref,
                   r_ref, lw_ref, k_ref, v_ref, kk_ref, a_ref, gate_ref, *, tm, seq):
    i = pl.program_id(0)
    g = g_ref[...]
    u = _rms(x_ref[...], g)
    not_first = jnp.where((i * tm) % seq == 0, 0.0, 1.0)
    up = _rms(xh_ref[HALO - 1:HALO, :], g) * not_first
    row = lax.broadcasted_iota(jnp.int32, (tm, 1), 0)
    x_prev = jnp.where(row == 0, up, pltpu.roll(u, 1, 0))
    xx = x_prev - u
    mix = lambda n: (u + xx * mu_ref[n:n + 1, :]).astype(BF16)
    w0, a0, k_k, k_a = (vec_ref[n:n + 1, :] for n in range(4))

    r = jnp.dot(mix(0), wr_ref[...], preferred_element_type=F32)
    k = jnp.dot(mix(2), wk_ref[...], preferred_element_type=F32)
    v = jnp.dot(mix(3), wv_ref[...], preferred_element_type=F32)
    dw = _dot(jnp.tanh(jnp.dot(mix(1), w1_ref[...], preferred_element_type=F32)), w2_ref[...])
    log_w = -_softplus(-(w0 + dw)) - 0.5
    a = _sigmoid(a0 + _dot(jnp.dot(mix(4), a1_ref[...], preferred_element_type=F32), a2_ref[...]))
    gate = _dot(_sigmoid(jnp.dot(mix(5), g1_ref[...], preferred_element_type=F32)), g2_ref[...])

    r_ref[...] = r.astype(r_ref.dtype)
    lw_ref[...] = -jnp.exp(log_w)
    k_ref[...] = (k * (1.0 + (a - 1.0) * k_a)).astype(k_ref.dtype)
    v_ref[...] = v.astype(v_ref.dtype)
    kk_ref[...] = (k * k_k).astype(kk_ref.dtype)
    a_ref[...] = a.astype(a_ref.dtype)
    gate_ref[...] = gate.astype(gate_ref.dtype)


def _rwkv_proj(h, g, mu, wr, wk, wv, w1, w2, a1, a2, g1, g2, vecs, seq):
    T, D = h.shape
    tm = RW_ROW_TILE
    full = lambda arr: pl.BlockSpec(arr.shape, lambda i: (0, 0), pipeline_mode=pl.Buffered(1))
    row = pl.BlockSpec((tm, D), lambda i: (i, 0))
    halo = pl.BlockSpec((HALO, D), lambda i: (jnp.maximum(i * (tm // HALO) - 1, 0), 0))
    dtypes = (BF16, F32, BF16, BF16, BF16, BF16, BF16)
    return pl.pallas_call(
        functools.partial(_rwproj_kernel, tm=tm, seq=seq),
        grid=(T // tm,),
        in_specs=[row, halo, full(g), full(mu), full(wr), full(wk), full(wv),
                  full(w1), full(w2), full(a1), full(a2), full(g1), full(g2), full(vecs)],
        out_specs=[row] * 7,
        out_shape=[jax.ShapeDtypeStruct((T, D), dt) for dt in dtypes],
        compiler_params=_params("parallel"),
        name="rwkv_proj",
    )(h, h, g, mu, wr, wk, wv, w1, w2, a1, a2, g1, g2, vecs)


def _bmm(a, b):
    return lax.dot_general(a.astype(BF16), b.astype(BF16), (((2,), (1,)), ((0,), (0,))),
                           preferred_element_type=F32)


def _bmm_nt(a, b):
    return lax.dot_general(a.astype(BF16), b.astype(BF16), (((2,), (2,)), ((0,), (0,))),
                           preferred_element_type=F32)


def _bmm_tn(a, b):
    return lax.dot_general(a.astype(BF16), b.astype(BF16), (((1,), (1,)), ((0,), (0,))),
                           preferred_element_type=F32)


def _rwscan_kernel(r_ref, lw_ref, k_ref, v_ref, kk_ref, a_ref, gate_ref,
                   rk_ref, lng_ref, lnb_ref, o_ref, s_ref, *, L, nsub, npair):
    @pl.when(pl.program_id(2) == 0)
    def _():
        s_ref[...] = jnp.zeros_like(s_ref)

    P = 2 * L
    lane = lax.broadcasted_iota(jnp.int32, (1, 1, LANES), 2)
    m0 = jnp.where(lane < RW_HEAD_DIM, 1.0, 0.0)
    m1 = 1.0 - m0
    ri = lax.broadcasted_iota(jnp.int32, (P, P), 0)
    ci = lax.broadcasted_iota(jnp.int32, (P, P), 1)
    same = (ri >> 6) == (ci >> 6)
    gsum = jnp.where(same, 1.0, 0.0).astype(BF16)
    strict = same & (ci < ri)
    incl = same & (ci <= ri)
    ri_l = lax.broadcasted_iota(jnp.int32, (L, L), 0)
    ci_l = lax.broadcasted_iota(jnp.int32, (L, L), 1)
    tril_l = jnp.where(ci_l <= ri_l, 1.0, 0.0).astype(BF16)

    nb = npair * nsub
    rows = nsub * L

    def pairs(x):
        return jnp.concatenate([x[:, p * LANES:(p + 1) * LANES] for p in range(npair)], axis=0)

    blk = lambda ref: pairs(ref[0].astype(F32)).reshape(nb, L, LANES)
    r, lw, k, v, kk, a = (blk(x) for x in (r_ref, lw_ref, k_ref, v_ref, kk_ref, a_ref))

    def lane_group_sum(x):
        return jnp.dot(x.astype(BF16), gsum, preferred_element_type=F32)

    ssq = lane_group_sum((kk * kk).reshape(nb * L, LANES)).reshape(nb, L, LANES)
    kap = kk * lax.rsqrt(jnp.maximum(ssq, 1e-24))
    bv = kap * a

    t1 = lw.astype(BF16)
    r1 = lw - t1.astype(F32)
    t2 = r1.astype(BF16)
    t3 = (r1 - t2.astype(F32)).astype(BF16)
    tril_b = jnp.broadcast_to(tril_l, (nb, L, L))
    cum = (_bmm(tril_b, t1) + _bmm(tril_b, t2)) + _bmm(tril_b, t3)
    c_last = cum[:, L - 1:L, :]
    w_incl = jnp.exp(cum)
    w_excl = jnp.exp(cum - lw)
    w_inv = jnp.exp(-cum)
    w_rem = jnp.exp(c_last - cum)
    w_last = jnp.exp(c_last)

    stack = lambda x: jnp.concatenate([x * m0, x * m1], axis=1)
    dup = lambda x: jnp.concatenate([x, x], axis=1)
    unstack = lambda x: x[:, :L, :] + x[:, L:, :]

    a_st = stack(-kap * w_excl)
    r_st = stack(r * w_incl)
    v_st = stack(v)
    gram = _bmm_nt(jnp.concatenate([a_st, r_st], axis=1),
                   jnp.concatenate([dup(bv * w_inv), dup(k * w_inv)], axis=1))
    a_ab = jnp.where(strict, gram[:, :P, :P], 0.0)
    a_ak = jnp.where(strict, gram[:, :P, P:], 0.0)
    a_rb = jnp.where(incl, gram[:, P:, :P], 0.0)
    a_rk = jnp.where(incl, gram[:, P:, P:], 0.0)

    tinv = jnp.where(ri == ci, 1.0, 0.0) + jnp.where((ri >> 1) == (ci >> 1), a_ab, 0.0)
    n = 2
    while n < L:
        sh = n.bit_length() - 1
        off = ((ri >> (sh + 1)) == (ci >> (sh + 1))) & ((ri >> sh) != (ci >> sh))
        e = jnp.where(off, a_ab, 0.0)
        tinv = tinv + _bmm(tinv, _bmm(e, tinv))
        n *= 2

    x1 = _bmm(a_ak, v_st)
    tu = _bmm(tinv, jnp.concatenate([x1, a_st], axis=2))
    u0_st, ta_st = tu[:, :, :LANES], tu[:, :, LANES:]
    yr = _bmm(a_rb, tu)
    y0 = unstack(yr[:, :, :LANES] + _bmm(a_rk, v_st))
    rq = unstack(r_st + yr[:, :, LANES:])
    bh_st = stack(bv * w_rem)
    kh_st = stack(k * w_rem)
    mn = _bmm_tn(bh_st, jnp.concatenate([ta_st, u0_st], axis=2))
    m_bd = mn[:, :, :LANES] + jnp.where(ri == ci, w_last, 0.0)
    n_bd = mn[:, :, LANES:] + _bmm_tn(kh_st, v_st)

    s = [s_ref[p] for p in range(npair)]
    ys = [[None] * nsub for _ in range(npair)]
    for c in range(nsub):
        for p in range(npair):
            i = p * nsub + c
            ys[p][c] = _dot(rq[i], s[p]) + y0[i]
            s[p] = _dot3(m_bd[i], s[p]) + n_bd[i]
    for p in range(npair):
        s_ref[p] = s[p]
    y = jnp.concatenate([jnp.concatenate(yp, axis=0) for yp in ys], axis=0)

    lanes = lambda x: jnp.concatenate([x[p * rows:(p + 1) * rows] for p in range(npair)], axis=1)
    inv_d = 1.0 / RW_HEAD_DIM
    mean = lane_group_sum(y) * inv_d
    d = y - mean
    var = lane_group_sum(d * d) * inv_d
    yn = lanes(d * lax.rsqrt(var + GN_EPS)) * lng_ref[...] + lnb_ref[...]
    r2, k2, v2 = (x[0].astype(F32) for x in (r_ref, k_ref, v_ref))
    bonus = lanes(lane_group_sum(pairs(r2 * k2 * rk_ref[...]))) * v2
    o_ref[0] = ((yn + bonus) * gate_ref[0].astype(F32)).astype(o_ref.dtype)


def _rwkv_scan(r, lw, k, v, kk, a, gate, r_k, ln_g, ln_b):
    B, S, D = r.shape
    L, lb, npair = RW_CHUNK, RW_BLOCK, RW_PAIRS
    width = npair * LANES
    blk = pl.BlockSpec((1, lb, width), lambda b, p, j: (b, j, p))
    vec = pl.BlockSpec((1, width), lambda b, p, j: (0, p))
    return pl.pallas_call(
        functools.partial(_rwscan_kernel, L=L, nsub=lb // L, npair=npair),
        grid=(B, D // width, S // lb),
        in_specs=[blk] * 7 + [vec] * 3,
        out_specs=blk,
        out_shape=jax.ShapeDtypeStruct((B, S, D), BF16),
        scratch_shapes=[pltpu.VMEM((npair, LANES, LANES), F32)],
        compiler_params=_params("parallel", "parallel", "arbitrary"),
        name="rwkv_scan",
    )(r, lw, k, v, kk, a, gate, r_k, ln_g, ln_b)


def _even_layer(h, B, S, g, w_in, b_if, conv_w, head_g, w_out, w_up, w_down):
    T, D = h.shape
    ncols = 3 * SB_WIDTH + 4 * ML_WIDTH
    w_main = w_in[:, :ncols].astype(BF16)
    w_if = jnp.pad(w_in[:, ncols:], ((0, 0), (0, LANES - 2 * ML_HEADS))).astype(BF16)
    sb_q, sb_k, sb_v, ml_qk, ml_v, ml_o, ml_if = _even_inproj(h, g[0:1], w_main, w_if)
    to3 = lambda t: t.reshape(B, S, t.shape[-1])
    a_out = _sb_attention(to3(sb_q), to3(sb_k), to3(sb_v))
    gates_nat = to3(ml_if)
    gates_t = jnp.swapaxes(gates_nat[:, :, :2 * ML_HEADS], 1, 2)
    bias_nat = jnp.pad(b_if, (0, LANES - 2 * ML_HEADS)).reshape(1, LANES)
    bias_t = jnp.broadcast_to(b_if[:, None], (2 * ML_HEADS, ML_CHUNK))
    cw = conv_w.reshape(CONV_WIDTH, 2 * ML_HEADS, LANES).transpose(1, 0, 2)
    hm = _mlstm(to3(ml_qk), to3(ml_v), to3(ml_o), gates_nat, gates_t, bias_nat, bias_t,
                cw, head_g.reshape(ML_HEADS, 1, LANES))
    return _mix_mlp([a_out.reshape(T, SB_WIDTH), hm.reshape(T, ML_WIDTH)],
                    w_out.astype(BF16), h, g, w_up.astype(BF16), w_down.astype(BF16))


def _odd_layer(h, B, S, g, mu, w_rkv, w0, w1, w2, a0, a1, a2, g1, g2, k_k, k_a, r_k,
               ln_g, ln_b, w_out, w_up, w_down):
    T, D = h.shape
    bf = lambda t: t.astype(BF16)
    mu8 = jnp.pad(mu, ((0, HALO - mu.shape[0]), (0, 0)))
    vecs = jnp.pad(jnp.stack([w0, a0, k_k, k_a]), ((0, HALO - 4), (0, 0)))
    r, lw, k, v, kk, a, gate = _rwkv_proj(
        h, g[0:1], mu8, bf(w_rkv[0]), bf(w_rkv[1]), bf(w_rkv[2]),
        bf(w1), bf(w2), bf(a1), bf(a2), bf(g1), bf(g2), vecs, S)
    to3 = lambda t: t.reshape(B, S, D)
    y = _rwkv_scan(to3(r), to3(lw), to3(k), to3(v), to3(kk), to3(a), to3(gate),
                   r_k.reshape(1, D), ln_g.reshape(1, D), ln_b.reshape(1, D))
    return _mix_mlp([y.reshape(T, D)], bf(w_out), h, g, bf(w_up), bf(w_down))


def kernel(x, norm_g, e_w_in, e_b_if, e_conv_w, e_head_g, e_w_out, r_mu, r_w_rkv, r_w0, r_w1, r_w2, r_a0, r_a1, r_a2, r_g1, r_g2, r_k_k, r_k_a, r_r_k, r_ln_g, r_ln_b, r_w_out, mlp_w_up, mlp_w_down):
    B, S, D = x.shape
    h = x.reshape(B * S, D)
    for layer in range(norm_g.shape[0]):
        g = norm_g[layer]
        if layer % 2 == 0:
            e = layer // 2
            h = _even_layer(h, B, S, g, e_w_in[e], e_b_if[e], e_conv_w[e], e_head_g[e],
                            e_w_out[e], mlp_w_up[layer], mlp_w_down[layer])
        else:
            o = layer // 2
            h = _odd_layer(h, B, S, g, r_mu[o], r_w_rkv[o], r_w0[o], r_w1[o], r_w2[o],
                           r_a0[o], r_a1[o], r_a2[o], r_g1[o], r_g2[o], r_k_k[o], r_k_a[o],
                           r_r_k[o], r_ln_g[o], r_ln_b[o], r_w_out[o],
                           mlp_w_up[layer], mlp_w_down[layer])
    return h.reshape(B, S, D)
```

```python
import functools
import math

import jax
import jax.numpy as jnp
from jax import lax
from jax.experimental import pallas as pl
from jax.experimental.pallas import tpu as pltpu

F32 = jnp.float32
BF16 = jnp.bfloat16

LANES = 128
V7X_VMEM_LIMIT_BYTES = 56 * 1024 * 1024

SB_HEADS = 8
SB_HEAD_DIM = 64
SB_WIDTH = SB_HEADS * SB_HEAD_DIM
ML_HEADS = 4
ML_HEAD_DIM = 128
ML_WIDTH = ML_HEADS * ML_HEAD_DIM
CONV_WIDTH = 4
RW_HEAD_DIM = 64
SB_Q_SCALE = -math.log2(math.e) / math.sqrt(SB_HEAD_DIM)
SB_LOG2_MASS_FLOOR = -160.0
NORM_EPS = 1e-6
GN_EPS = 64e-5

ROW_TILE = 512
RW_ROW_TILE = 512
MLP_ROW_TILE = 1024
MLP_FF_TILE = 1024
SB_Q_TILE = 256
SB_K_TILE = 256
ML_CHUNK = 256
RW_CHUNK = 64
RW_BLOCK = 512
RW_PAIRS = 4
HALO = 8


def _params(*sem):
    return pltpu.CompilerParams(dimension_semantics=sem,
                                vmem_limit_bytes=V7X_VMEM_LIMIT_BYTES)


def _dot(a, b):
    return jnp.dot(a.astype(BF16), b.astype(BF16), preferred_element_type=F32)


def _dot_nt(a, b):
    return lax.dot_general(a.astype(BF16), b.astype(BF16), (((1,), (1,)), ((), ())),
                           preferred_element_type=F32)


def _dot_tn(a, b):
    return lax.dot_general(a.astype(BF16), b.astype(BF16), (((0,), (0,)), ((), ())),
                           preferred_element_type=F32)


def _split(x):
    hi = x.astype(BF16)
    lo = (x - hi.astype(F32)).astype(BF16)
    return hi, lo


def _dot_x_exact(x, m):
    hi, lo = _split(x)
    return (jnp.dot(hi, m, preferred_element_type=F32)
            + jnp.dot(lo, m, preferred_element_type=F32))


def _dot_exact_x(m, x):
    hi, lo = _split(x)
    return (jnp.dot(m, hi, preferred_element_type=F32)
            + jnp.dot(m, lo, preferred_element_type=F32))


def _rms(x, g):
    ms = jnp.mean(x * x, axis=-1, keepdims=True)
    return x * lax.rsqrt(ms + NORM_EPS) * g


def _softplus(z):
    return jnp.maximum(z, 0.0) + jnp.log(1.0 + jnp.exp(-jnp.abs(z)))


def _neg_abs(x):
    bits = lax.bitcast_convert_type(x, jnp.uint32) | jnp.uint32(0x80000000)
    return lax.bitcast_convert_type(bits, F32)


def _sigmoid(z):
    return 1.0 / (1.0 + jnp.exp(-z))


def _inproj_kernel(x_ref, g_ref, w_ref, wif_ref, *out_refs, widths):
    u = _rms(x_ref[...], g_ref[...]).astype(BF16)
    c0 = 0
    for n, (o_ref, w) in enumerate(zip(out_refs[:-1], widths)):
        t = jnp.dot(u, w_ref[:, c0:c0 + w], preferred_element_type=F32)
        if n == 0:
            t = t * SB_Q_SCALE
        o_ref[...] = t.astype(o_ref.dtype)
        c0 += w
    out_refs[-1][...] = jnp.dot(u, wif_ref[...], preferred_element_type=F32)


def _even_inproj(h, g, w_main, w_if):
    T, D = h.shape
    widths = (SB_WIDTH, SB_WIDTH, SB_WIDTH, 2 * ML_WIDTH, ML_WIDTH, ML_WIDTH)
    tm = ROW_TILE
    out_shape = [jax.ShapeDtypeStruct((T, w), BF16) for w in widths]
    out_shape.append(jax.ShapeDtypeStruct((T, LANES), F32))
    out_specs = [pl.BlockSpec((tm, w), lambda i: (i, 0)) for w in widths]
    out_specs.append(pl.BlockSpec((tm, LANES), lambda i: (i, 0)))
    return pl.pallas_call(
        functools.partial(_inproj_kernel, widths=widths),
        grid=(T // tm,),
        in_specs=[pl.BlockSpec((tm, D), lambda i: (i, 0)),
                  pl.BlockSpec((1, D), lambda i: (0, 0)),
                  pl.BlockSpec(w_main.shape, lambda i: (0, 0)),
                  pl.BlockSpec(w_if.shape, lambda i: (0, 0))],
        out_specs=out_specs,
        out_shape=out_shape,
        compiler_params=_params("parallel"),
        name="even_inproj",
    )(h, g, w_main, w_if)


def _sb_kernel(q_ref, k_ref, v_ref, o_ref, *, tq, tk):
    qi = pl.program_id(2)
    q2 = q_ref[0]
    lane = lax.broadcasted_iota(jnp.int32, (1, LANES), 1)
    first_head = lane < SB_HEAD_DIM
    rr = lax.broadcasted_iota(jnp.int32, (tk, tk), 0)
    cc = lax.broadcasted_iota(jnp.int32, (tk, tk), 1)
    cum_mat = jnp.where(rr > cc, 1.0, 0.0).astype(BF16)
    row_t = lax.broadcasted_iota(jnp.int32, (tq, tk), 0)
    col_s = lax.broadcasted_iota(jnp.int32, (tq, tk), 1)
    nsub = tq // tk
    heads = (first_head, jnp.logical_not(first_head))
    qhs = [jnp.where(keep, q2, jnp.zeros_like(q2)) for keep in heads]

    def weights(qh, ks, carry, diagonal):
        zn = lax.dot_general(qh, ks, (((1,), (1,)), ((), ())), preferred_element_type=F32)
        ps = [None] * nsub
        for sub in range(nsub - 1, -1, -1):
            zs = zn[:, sub * tk:(sub + 1) * tk]
            lk = jnp.minimum(zs, 0.0) - jnp.log2(1.0 + jnp.exp2(_neg_abs(zs)))
            if diagonal:
                causal = (col_s + sub * tk) < row_t
                lk = jnp.where(causal, lk, 0.0)
            lkb = lk.astype(BF16)
            cs = jnp.dot(lkb, cum_mat, preferred_element_type=F32)
            p = jnp.exp2(lk - zs + cs + carry)
            if diagonal:
                p = jnp.where(causal, p, 0.0)
            ps[sub] = p.astype(BF16)
            carry = carry + (cs[:, 0:1] + lkb[:, 0:1].astype(F32))
        return jnp.concatenate(ps, axis=1), carry

    def span(start, state, diagonal):
        c0, c1, acc = state
        ks = k_ref[0, pl.ds(start, tq), :]
        vs = v_ref[0, pl.ds(start, tq), :]
        p0, c0 = weights(qhs[0], ks, c0, diagonal)
        p1, c1 = weights(qhs[1], ks, c1, diagonal)
        v01 = jnp.concatenate([jnp.where(keep, vs, jnp.zeros_like(vs)) for keep in heads], axis=0)
        acc = acc + jnp.dot(jnp.concatenate([p0, p1], axis=1), v01, preferred_element_type=F32)
        return c0, c1, acc

    col0 = jnp.zeros((tq, 1), F32)
    zero = (col0, col0, jnp.zeros((tq, LANES), F32))
    start0 = pl.multiple_of(qi * tq, tq)

    def with_previous_span(_):
        st = span(start0, zero, True)
        return span(pl.multiple_of(start0 - tq, tq), st, False)

    def diagonal_only(_):
        return span(start0, zero, True)

    state = lax.cond(qi > 0, with_previous_span, diagonal_only, None)

    def mass_left(st):
        return (jnp.max(jnp.maximum(st[0], st[1])) > SB_LOG2_MASS_FLOOR).astype(jnp.int32)

    def cond(c):
        return (c[0] < qi) & (c[1] > 0)

    def body(c):
        j, _, st = c
        st = span(pl.multiple_of((qi - 1 - j) * tq, tq), st, False)
        return j + 1, mass_left(st), st

    _, _, state = lax.while_loop(cond, body, (jnp.int32(1), mass_left(state), state))
    o_ref[0] = state[2].astype(o_ref.dtype)


def _sb_attention(q, k, v):
    B, S, W = q.shape
    tq, tk = SB_Q_TILE, SB_K_TILE
    npair = W // LANES
    return pl.pallas_call(
        functools.partial(_sb_kernel, tq=tq, tk=tk),
        grid=(B, npair, S // tq),
        in_specs=[pl.BlockSpec((1, tq, LANES), lambda b, p, i: (b, i, p)),
                  pl.BlockSpec((1, S, LANES), lambda b, p, i: (b, 0, p)),
                  pl.BlockSpec((1, S, LANES), lambda b, p, i: (b, 0, p))],
        out_specs=pl.BlockSpec((1, tq, LANES), lambda b, p, i: (b, i, p)),
        out_shape=jax.ShapeDtypeStruct((B, S, W), BF16),
        compiler_params=_params("parallel", "parallel", "arbitrary"),
        name="sb_attention",
    )(q, k, v)


def _mlstm_kernel(q_ref, k_ref, v_ref, o_ref, gn_ref, gt_ref, bn_ref, bt_ref,
                  cwq_ref, cwk_ref, hg_ref, out_ref,
                  c_ref, m_ref, pq_ref, pk_ref, *, L):
    chunk = pl.program_id(1)

    @pl.when(chunk == 0)
    def _():
        c_ref[...] = jnp.zeros_like(c_ref)
        m_ref[...] = jnp.zeros_like(m_ref)
        pq_ref[...] = jnp.zeros_like(pq_ref)
        pk_ref[...] = jnp.zeros_like(pk_ref)

    def conv_silu(x, prev, w):
        xf = jnp.concatenate([prev, x], axis=0)
        y = xf * w[CONV_WIDTH - 1:CONV_WIDTH, :]
        for j in range(1, CONV_WIDTH):
            y = y + pltpu.roll(xf, j, 0) * w[CONV_WIDTH - 1 - j:CONV_WIDTH - j, :]
        y = y[HALO:, :]
        return y * _sigmoid(y)

    ri = lax.broadcasted_iota(jnp.int32, (L, L), 0)
    ci = lax.broadcasted_iota(jnp.int32, (L, L), 1)
    tril = ci <= ri
    tril_m = jnp.where(tril, 1.0, 0.0).astype(BF16)
    triu_m = jnp.where(ci >= ri, 1.0, 0.0).astype(BF16)
    gn = gn_ref[0] + bn_ref[...]
    gt = gt_ref[0] + bt_ref[...]
    ones = jnp.ones((L, LANES), BF16)

    def one_head(head):
        sl = slice(head * LANES, (head + 1) * LANES)
        q_raw = q_ref[0, :, sl].astype(F32)
        k_raw = k_ref[0, :, sl].astype(F32)
        q = conv_silu(q_raw, pq_ref[:, sl], cwq_ref[head])
        k = conv_silu(k_raw, pk_ref[:, sl], cwk_ref[head + ML_HEADS]) * (ML_HEAD_DIM ** -0.5)
        pq_ref[:, sl] = q_raw[L - HALO:, :]
        pk_ref[:, sl] = k_raw[L - HALO:, :]
        v = v_ref[0, :, sl]

        li_col = gn[:, head:head + 1]
        lf_col = -_softplus(-gn[:, head + ML_HEADS:head + ML_HEADS + 1])
        li_row = gt[head:head + 1, :]
        lf_row = -_softplus(-gt[head + ML_HEADS:head + ML_HEADS + 1, :])
        b_col = _dot_exact_x(tril_m, jnp.broadcast_to(lf_col, (L, LANES)))[:, :1]
        b_row = _dot_x_exact(jnp.broadcast_to(lf_row, (HALO, L)), triu_m)[:1, :]

        m_prev = m_ref[head, 0:1, 0:1]
        dmat = jnp.where(tril, b_col - b_row + li_row, -jnp.inf)
        inter = b_col + m_prev
        m_t = jnp.maximum(jnp.max(dmat, axis=1, keepdims=True), inter)
        scores = _dot_nt(q, k) * jnp.exp(dmat - m_t)
        w_inter = jnp.exp(inter - m_t)
        v_aug = jnp.concatenate([v, ones], axis=1)
        c_aug = c_ref[head]
        num_aug = _dot(scores, v_aug) + w_inter * _dot(q, c_aug)
        num = num_aug[:, :LANES]
        den = num_aug[:, LANES:LANES + 1]
        hval = num / jnp.maximum(jnp.abs(den), jnp.exp(-m_t))

        b_last = b_col[L - 1:L, :]
        gcol = b_last - b_col + li_col
        m_new = jnp.maximum(b_last + m_prev, jnp.max(gcol, axis=0, keepdims=True))
        w_state = jnp.exp(b_last + m_prev - m_new)
        w_tok = jnp.exp(gcol - m_new)
        c_ref[head] = w_state * c_aug + _dot_tn(k * w_tok, v_aug)
        m_ref[head] = jnp.broadcast_to(m_new, (HALO, LANES))

        hn = hval * lax.rsqrt(jnp.mean(hval * hval, axis=-1, keepdims=True) + NORM_EPS) * hg_ref[head]
        out_ref[0, :, sl] = (hn * _sigmoid(o_ref[0, :, sl].astype(F32))).astype(out_ref.dtype)

    for head in range(ML_HEADS):
        one_head(head)


def _mlstm(qk, v, o, gates_nat, gates_t, bias_nat, bias_t, conv_w, head_g):
    B, S, W = v.shape
    L = ML_CHUNK
    blk = lambda off: pl.BlockSpec((1, L, W), lambda b, c: (b, c, off))
    full = lambda arr: pl.BlockSpec(arr.shape, lambda b, c: (0,) * arr.ndim)
    return pl.pallas_call(
        functools.partial(_mlstm_kernel, L=L),
        grid=(B, S // L),
        in_specs=[blk(0), blk(1), blk(0), blk(0),
                  pl.BlockSpec((1, L, LANES), lambda b, c: (b, c, 0)),
                  pl.BlockSpec((1, 2 * ML_HEADS, L), lambda b, c: (b, 0, c)),
                  full(bias_nat), full(bias_t), full(conv_w), full(conv_w), full(head_g)],
        out_specs=pl.BlockSpec((1, L, W), lambda b, c: (b, c, 0)),
        out_shape=jax.ShapeDtypeStruct((B, S, W), BF16),
        scratch_shapes=[pltpu.VMEM((ML_HEADS, ML_HEAD_DIM, 2 * LANES), F32),
                        pltpu.VMEM((ML_HEADS, HALO, LANES), F32),
                        pltpu.VMEM((HALO, W), F32),
                        pltpu.VMEM((HALO, W), F32)],
        compiler_params=_params("parallel", "arbitrary"),
        name="mlstm",
    )(qk, qk, v, o, gates_nat, gates_t, bias_nat, bias_t, conv_w, conv_w, head_g)


def _mixmlp_kernel(*refs, nparts):
    parts = refs[:nparts]
    wo_ref, h_ref, g_ref, wu_ref, wd_ref, o_ref, u_ref, acc_ref = refs[nparts:]
    j = pl.program_id(1)

    @pl.when(j == 0)
    def _():
        c0 = 0
        mix = None
        for p in parts:
            w = p.shape[1]
            t = jnp.dot(p[...], wo_ref[c0:c0 + w, :], preferred_element_type=F32)
            mix = t if mix is None else mix + t
            c0 += w
        h1 = h_ref[...] + _rms(mix, g_ref[1:2, :])
        o_ref[...] = h1
        u_ref[...] = _rms(h1, g_ref[2:3, :]).astype(BF16)
        acc_ref[...] = jnp.zeros_like(acc_ref)

    a = jnp.dot(u_ref[...], wu_ref[...], preferred_element_type=F32)
    a = jnp.maximum(a, 0.0)
    acc_ref[...] += jnp.dot((a * a).astype(BF16), wd_ref[...], preferred_element_type=F32)

    @pl.when(j == pl.num_programs(1) - 1)
    def _():
        o_ref[...] = o_ref[...] + _rms(acc_ref[...], g_ref[3:4, :])


def _mix_mlp(parts, w_out, h, g, w_up, w_down):
    T, D = h.shape
    FF = w_up.shape[1]
    tm, tf = MLP_ROW_TILE, MLP_FF_TILE
    in_specs = [pl.BlockSpec((tm, p.shape[1]), lambda i, j: (i, 0)) for p in parts]
    in_specs += [pl.BlockSpec(w_out.shape, lambda i, j: (0, 0), pipeline_mode=pl.Buffered(1)),
                 pl.BlockSpec((tm, D), lambda i, j: (i, 0)),
                 pl.BlockSpec(g.shape, lambda i, j: (0, 0)),
                 pl.BlockSpec((D, tf), lambda i, j: (0, j)),
                 pl.BlockSpec((tf, D), lambda i, j: (j, 0))]
    return pl.pallas_call(
        functools.partial(_mixmlp_kernel, nparts=len(parts)),
        grid=(T // tm, FF // tf),
        in_specs=in_specs,
        out_specs=pl.BlockSpec((tm, D), lambda i, j: (i, 0)),
        out_shape=jax.ShapeDtypeStruct((T, D), F32),
        scratch_shapes=[pltpu.VMEM((tm, D), BF16), pltpu.VMEM((tm, D), F32)],
        compiler_params=_params("parallel", "arbitrary"),
        name="mix_mlp",
    )(*parts, w_out, h, g, w_up, w_down)


def _rwproj_kernel(x_ref, xh_ref, g_ref, mu_ref, wr_ref, wk_ref, wv_ref,
                   w1_ref, w2_ref, a1_ref, a2_ref, g1_ref, g2_ref, vec_ref,
                   r_ref, lw_ref, k_ref, v_ref, kk_ref, a_ref, gate_ref, *, tm, seq):
    i = pl.program_id(0)
    g = g_ref[...]
    u = _rms(x_ref[...], g)
    not_first = jnp.where((i * tm) % seq == 0, 0.0, 1.0)
    up = _rms(xh_ref[HALO - 1:HALO, :], g) * not_first
    row = lax.broadcasted_iota(jnp.int32, (tm, 1), 0)
    x_prev = jnp.where(row == 0, up, pltpu.roll(u, 1, 0))
    xx = x_prev - u
    mix = lambda n: (u + xx * mu_ref[n:n + 1, :]).astype(BF16)
    w0, a0, k_k, k_a = (vec_ref[n:n + 1, :] for n in range(4))

    r = jnp.dot(mix(0), wr_ref[...], preferred_element_type=F32)
    k = jnp.dot(mix(2), wk_ref[...], preferred_element_type=F32)
    v = jnp.dot(mix(3), wv_ref[...], preferred_element_type=F32)
    dw = _dot(jnp.tanh(jnp.dot(mix(1), w1_ref[...], preferred_element_type=F32)), w2_ref[...])
    log_w = -_softplus(-(w0 + dw)) - 0.5
    a = _sigmoid(a0 + _dot(jnp.dot(mix(4), a1_ref[...], preferred_element_type=F32), a2_ref[...]))
    gate = _dot(_sigmoid(jnp.dot(mix(5), g1_ref[...], preferred_element_type=F32)), g2_ref[...])

    r_ref[...] = r.astype(r_ref.dtype)
    lw_ref[...] = -jnp.exp(log_w)
    k_ref[...] = (k * (1.0 + (a - 1.0) * k_a)).astype(k_ref.dtype)
    v_ref[...] = v.astype(v_ref.dtype)
    kk_ref[...] = (k * k_k).astype(kk_ref.dtype)
    a_ref[...] = a.astype(a_ref.dtype)
    gate_ref[...] = gate.astype(gate_ref.dtype)


def _rwkv_proj(h, g, mu, wr, wk, wv, w1, w2, a1, a2, g1, g2, vecs, seq):
    T, D = h.shape
    tm = RW_ROW_TILE
    full = lambda arr: pl.BlockSpec(arr.shape, lambda i: (0, 0), pipeline_mode=pl.Buffered(1))
    row = pl.BlockSpec((tm, D), lambda i: (i, 0))
    halo = pl.BlockSpec((HALO, D), lambda i: (jnp.maximum(i * (tm // HALO) - 1, 0), 0))
    dtypes = (BF16, F32, BF16, BF16, BF16, BF16, BF16)
    return pl.pallas_call(
        functools.partial(_rwproj_kernel, tm=tm, seq=seq),
        grid=(T // tm,),
        in_specs=[row, halo, full(g), full(mu), full(wr), full(wk), full(wv),
                  full(w1), full(w2), full(a1), full(a2), full(g1), full(g2), full(vecs)],
        out_specs=[row] * 7,
        out_shape=[jax.ShapeDtypeStruct((T, D), dt) for dt in dtypes],
        compiler_params=_params("parallel"),
        name="rwkv_proj",
    )(h, h, g, mu, wr, wk, wv, w1, w2, a1, a2, g1, g2, vecs)


def _bmm(a, b):
    return lax.dot_general(a.astype(BF16), b.astype(BF16), (((2,), (1,)), ((0,), (0,))),
                           preferred_element_type=F32)


def _bmm_nt(a, b):
    return lax.dot_general(a.astype(BF16), b.astype(BF16), (((2,), (2,)), ((0,), (0,))),
                           preferred_element_type=F32)


def _bmm_tn(a, b):
    return lax.dot_general(a.astype(BF16), b.astype(BF16), (((1,), (1,)), ((0,), (0,))),
                           preferred_element_type=F32)


def _rwscan_kernel(r_ref, lw_ref, k_ref, v_ref, kk_ref, a_ref, gate_ref,
                   rk_ref, lng_ref, lnb_ref, o_ref, s_ref, *, L, nsub, npair):
    @pl.when(pl.program_id(2) == 0)
    def _():
        s_ref[...] = jnp.zeros_like(s_ref)

    P = 2 * L
    lane = lax.broadcasted_iota(jnp.int32, (1, 1, LANES), 2)
    m0 = jnp.where(lane < RW_HEAD_DIM, 1.0, 0.0)
    m1 = 1.0 - m0
    ri = lax.broadcasted_iota(jnp.int32, (P, P), 0)
    ci = lax.broadcasted_iota(jnp.int32, (P, P), 1)
    same = (ri >> 6) == (ci >> 6)
    gsum = jnp.where(same, 1.0, 0.0).astype(BF16)
    strict = same & (ci < ri)
    incl = same & (ci <= ri)
    ri_l = lax.broadcasted_iota(jnp.int32, (L, L), 0)
    ci_l = lax.broadcasted_iota(jnp.int32, (L, L), 1)
    tril_l = jnp.where(ci_l <= ri_l, 1.0, 0.0).astype(BF16)

    nb = npair * nsub
    rows = nsub * L

    def pairs(x):
        return jnp.concatenate([x[:, p * LANES:(p + 1) * LANES] for p in range(npair)], axis=0)

    blk = lambda ref: pairs(ref[0].astype(F32)).reshape(nb, L, LANES)
    r, lw, k, v, kk, a = (blk(x) for x in (r_ref, lw_ref, k_ref, v_ref, kk_ref, a_ref))

    def lane_group_sum(x):
        return jnp.dot(x.astype(BF16), gsum, preferred_element_type=F32)

    ssq = lane_group_sum((kk * kk).reshape(nb * L, LANES)).reshape(nb, L, LANES)
    kap = kk * lax.rsqrt(jnp.maximum(ssq, 1e-24))
    bv = kap * a

    t1, t2 = _split(lw)
    tril_b = jnp.broadcast_to(tril_l, (nb, L, L))
    cum = _bmm(tril_b, t1) + _bmm(tril_b, t2)
    c_last = cum[:, L - 1:L, :]
    w_incl = jnp.exp(cum)
    w_excl = jnp.exp(cum - lw)
    w_inv = jnp.exp(-cum)
    w_rem = jnp.exp(c_last - cum)
    w_last = jnp.exp(c_last)

    stack = lambda x: jnp.concatenate([x * m0, x * m1], axis=1)
    dup = lambda x: jnp.concatenate([x, x], axis=1)
    unstack = lambda x: x[:, :L, :] + x[:, L:, :]

    a_st = stack(-kap * w_excl)
    r_st = stack(r * w_incl)
    v_st = stack(v)
    gram = _bmm_nt(jnp.concatenate([a_st, r_st], axis=1),
                   jnp.concatenate([dup(bv * w_inv), dup(k * w_inv)], axis=1))
    a_ab = jnp.where(strict, gram[:, :P, :P], 0.0)
    a_ak = jnp.where(strict, gram[:, :P, P:], 0.0)
    a_rb = jnp.where(incl, gram[:, P:, :P], 0.0)
    a_rk = jnp.where(incl, gram[:, P:, P:], 0.0)

    tinv = jnp.where(ri == ci, 1.0, 0.0) + jnp.where((ri >> 1) == (ci >> 1), a_ab, 0.0)
    n = 2
    while n < L:
        sh = n.bit_length() - 1
        off = ((ri >> (sh + 1)) == (ci >> (sh + 1))) & ((ri >> sh) != (ci >> sh))
        e = jnp.where(off, a_ab, 0.0)
        tinv = tinv + _bmm(tinv, _bmm(e, tinv))
        n *= 2

    x1 = _bmm(a_ak, v_st)
    tu = _bmm(tinv, jnp.concatenate([x1, a_st], axis=2))
    u0_st, ta_st = tu[:, :, :LANES], tu[:, :, LANES:]
    yr = _bmm(a_rb, tu)
    y0 = unstack(yr[:, :, :LANES] + _bmm(a_rk, v_st))
    rq = unstack(r_st + yr[:, :, LANES:])
    bh_st = stack(bv * w_rem)
    kh_st = stack(k * w_rem)
    mn = _bmm_tn(bh_st, jnp.concatenate([ta_st, u0_st], axis=2))
    m_bd = mn[:, :, :LANES] + jnp.where(ri == ci, w_last, 0.0)
    n_bd = mn[:, :, LANES:] + _bmm_tn(kh_st, v_st)

    s = [s_ref[p] for p in range(npair)]
    ys = [[None] * nsub for _ in range(npair)]
    for c in range(nsub):
        for p in range(npair):
            i = p * nsub + c
            ys[p][c] = _dot(rq[i], s[p]) + y0[i]
            s[p] = _dot_exact_x(m_bd[i].astype(BF16), s[p]) + n_bd[i]
    for p in range(npair):
        s_ref[p] = s[p]
    y = jnp.concatenate([jnp.concatenate(yp, axis=0) for yp in ys], axis=0)

    lanes = lambda x: jnp.concatenate([x[p * rows:(p + 1) * rows] for p in range(npair)], axis=1)
    inv_d = 1.0 / RW_HEAD_DIM
    mean = lane_group_sum(y) * inv_d
    d = y - mean
    var = lane_group_sum(d * d) * inv_d
    yn = lanes(d * lax.rsqrt(var + GN_EPS)) * lng_ref[...] + lnb_ref[...]
    r2, k2, v2 = (x[0].astype(F32) for x in (r_ref, k_ref, v_ref))
    bonus = lanes(lane_group_sum(pairs(r2 * k2 * rk_ref[...]))) * v2
    o_ref[0] = ((yn + bonus) * gate_ref[0].astype(F32)).astype(o_ref.dtype)


def _rwkv_scan(r, lw, k, v, kk, a, gate, r_k, ln_g, ln_b):
    B, S, D = r.shape
    L, lb, npair = RW_CHUNK, RW_BLOCK, RW_PAIRS
    width = npair * LANES
    blk = pl.BlockSpec((1, lb, width), lambda b, p, j: (b, j, p))
    vec = pl.BlockSpec((1, width), lambda b, p, j: (0, p))
    return pl.pallas_call(
        functools.partial(_rwscan_kernel, L=L, nsub=lb // L, npair=npair),
        grid=(B, D // width, S // lb),
        in_specs=[blk] * 7 + [vec] * 3,
        out_specs=blk,
        out_shape=jax.ShapeDtypeStruct((B, S, D), BF16),
        scratch_shapes=[pltpu.VMEM((npair, LANES, LANES), F32)],
        compiler_params=_params("parallel", "parallel", "arbitrary"),
        name="rwkv_scan",
    )(r, lw, k, v, kk, a, gate, r_k, ln_g, ln_b)


def _even_layer(h, B, S, g, w_in, b_if, conv_w, head_g, w_out, w_up, w_down):
    T, D = h.shape
    ncols = 3 * SB_WIDTH + 4 * ML_WIDTH
    w_main = w_in[:, :ncols].astype(BF16)
    w_if = jnp.pad(w_in[:, ncols:], ((0, 0), (0, LANES - 2 * ML_HEADS))).astype(BF16)
    sb_q, sb_k, sb_v, ml_qk, ml_v, ml_o, ml_if = _even_inproj(h, g[0:1], w_main, w_if)
    to3 = lambda t: t.reshape(B, S, t.shape[-1])
    a_out = _sb_attention(to3(sb_q), to3(sb_k), to3(sb_v))
    gates_nat = to3(ml_if)
    gates_t = jnp.swapaxes(gates_nat[:, :, :2 * ML_HEADS], 1, 2)
    bias_nat = jnp.pad(b_if, (0, LANES - 2 * ML_HEADS)).reshape(1, LANES)
    bias_t = jnp.broadcast_to(b_if[:, None], (2 * ML_HEADS, ML_CHUNK))
    cw = conv_w.reshape(CONV_WIDTH, 2 * ML_HEADS, LANES).transpose(1, 0, 2)
    hm = _mlstm(to3(ml_qk), to3(ml_v), to3(ml_o), gates_nat, gates_t, bias_nat, bias_t,
                cw, head_g.reshape(ML_HEADS, 1, LANES))
    return _mix_mlp([a_out.reshape(T, SB_WIDTH), hm.reshape(T, ML_WIDTH)],
                    w_out.astype(BF16), h, g, w_up.astype(BF16), w_down.astype(BF16))


def _odd_layer(h, B, S, g, mu, w_rkv, w0, w1, w2, a0, a1, a2, g1, g2, k_k, k_a, r_k,
               ln_g, ln_b, w_out, w_up, w_down):
    T, D = h.shape
    bf = lambda t: t.astype(BF16)
    mu8 = jnp.pad(mu, ((0, HALO - mu.shape[0]), (0, 0)))
    vecs = jnp.pad(jnp.stack([w0, a0, k_k, k_a]), ((0, HALO - 4), (0, 0)))
    r, lw, k, v, kk, a, gate = _rwkv_proj(
        h, g[0:1], mu8, bf(w_rkv[0]), bf(w_rkv[1]), bf(w_rkv[2]),
        bf(w1), bf(w2), bf(a1), bf(a2), bf(g1), bf(g2), vecs, S)
    to3 = lambda t: t.reshape(B, S, D)
    y = _rwkv_scan(to3(r), to3(lw), to3(k), to3(v), to3(kk), to3(a), to3(gate),
                   r_k.reshape(1, D), ln_g.reshape(1, D), ln_b.reshape(1, D))
    return _mix_mlp([y.reshape(T, D)], bf(w_out), h, g, bf(w_up), bf(w_down))


def kernel(x, norm_g, e_w_in, e_b_if, e_conv_w, e_head_g, e_w_out, r_mu, r_w_rkv, r_w0, r_w1, r_w2, r_a0, r_a1, r_a2, r_g1, r_g2, r_k_k, r_k_a, r_r_k, r_ln_g, r_ln_b, r_w_out, mlp_w_up, mlp_w_down):
    B, S, D = x.shape
    h = x.reshape(B * S, D)
    for layer in range(norm_g.shape[0]):
        g = norm_g[layer]
        if layer % 2 == 0:
            e = layer // 2
            h = _even_layer(h, B, S, g, e_w_in[e], e_b_if[e], e_conv_w[e], e_head_g[e],
                            e_w_out[e], mlp_w_up[layer], mlp_w_down[layer])
        else:
            o = layer // 2
            h = _odd_layer(h, B, S, g, r_mu[o], r_w_rkv[o], r_w0[o], r_w1[o], r_w2[o],
                           r_a0[o], r_a1[o], r_a2[o], r_g1[o], r_g2[o], r_k_k[o], r_k_a[o],
                           r_r_k[o], r_ln_g[o], r_ln_b[o], r_w_out[o],
                           mlp_w_up[layer], mlp_w_down[layer])
    return h.reshape(B, S, D)
```

```python
import functools
import math

import jax
import jax.numpy as jnp
from jax import lax
from jax.experimental import pallas as pl
from jax.experimental.pallas import tpu as pltpu

F32 = jnp.float32
BF16 = jnp.bfloat16

LANES = 128
V7X_VMEM_LIMIT_BYTES = 56 * 1024 * 1024

SB_HEADS = 8
SB_HEAD_DIM = 64
SB_WIDTH = SB_HEADS * SB_HEAD_DIM
ML_HEADS = 4
ML_HEAD_DIM = 128
ML_WIDTH = ML_HEADS * ML_HEAD_DIM
CONV_WIDTH = 4
RW_HEAD_DIM = 64
SB_Q_SCALE = -math.log2(math.e) / math.sqrt(SB_HEAD_DIM)
SB_LOG2_MASS_FLOOR = -160.0
NORM_EPS = 1e-6
GN_EPS = 64e-5

ROW_TILE = 512
RW_ROW_TILE = 512
MLP_ROW_TILE = 1024
MLP_ROW_SUB = 256
MLP_FF_TILE = 1024
SB_Q_TILE = 256
SB_K_TILE = 256
ML_CHUNK = 256
RW_CHUNK = 64
RW_BLOCK = 256
RW_PAIRS = 8
HALO = 8


def _params(*sem):
    return pltpu.CompilerParams(dimension_semantics=sem,
                                vmem_limit_bytes=V7X_VMEM_LIMIT_BYTES)


def _dot(a, b):
    return jnp.dot(a.astype(BF16), b.astype(BF16), preferred_element_type=F32)


def _dot_nt(a, b):
    return lax.dot_general(a.astype(BF16), b.astype(BF16), (((1,), (1,)), ((), ())),
                           preferred_element_type=F32)


def _dot_tn(a, b):
    return lax.dot_general(a.astype(BF16), b.astype(BF16), (((0,), (0,)), ((), ())),
                           preferred_element_type=F32)


def _split(x):
    hi = x.astype(BF16)
    lo = (x - hi.astype(F32)).astype(BF16)
    return hi, lo


def _dot_x_exact(x, m):
    hi, lo = _split(x)
    return (jnp.dot(hi, m, preferred_element_type=F32)
            + jnp.dot(lo, m, preferred_element_type=F32))


def _dot_exact_x(m, x):
    hi, lo = _split(x)
    return (jnp.dot(m, hi, preferred_element_type=F32)
            + jnp.dot(m, lo, preferred_element_type=F32))


def _rms(x, g):
    ms = jnp.mean(x * x, axis=-1, keepdims=True)
    return x * lax.rsqrt(ms + NORM_EPS) * g


def _softplus(z):
    return jnp.maximum(z, 0.0) + jnp.log(1.0 + jnp.exp(-jnp.abs(z)))


def _neg_abs(x):
    bits = lax.bitcast_convert_type(x, jnp.uint32) | jnp.uint32(0x80000000)
    return lax.bitcast_convert_type(bits, F32)


def _sigmoid(z):
    return 1.0 / (1.0 + jnp.exp(-z))


def _inproj_kernel(x_ref, g_ref, w_ref, wif_ref, *out_refs, widths):
    u = _rms(x_ref[...], g_ref[...]).astype(BF16)
    c0 = 0
    for n, (o_ref, w) in enumerate(zip(out_refs[:-1], widths)):
        t = jnp.dot(u, w_ref[:, c0:c0 + w], preferred_element_type=F32)
        if n == 0:
            t = t * SB_Q_SCALE
        o_ref[...] = t.astype(o_ref.dtype)
        c0 += w
    out_refs[-1][...] = jnp.dot(u, wif_ref[...], preferred_element_type=F32)


def _even_inproj(h, g, w_main, w_if):
    T, D = h.shape
    widths = (SB_WIDTH, SB_WIDTH, SB_WIDTH, 2 * ML_WIDTH, ML_WIDTH, ML_WIDTH)
    tm = ROW_TILE
    out_shape = [jax.ShapeDtypeStruct((T, w), BF16) for w in widths]
    out_shape.append(jax.ShapeDtypeStruct((T, LANES), F32))
    out_specs = [pl.BlockSpec((tm, w), lambda i: (i, 0)) for w in widths]
    out_specs.append(pl.BlockSpec((tm, LANES), lambda i: (i, 0)))
    return pl.pallas_call(
        functools.partial(_inproj_kernel, widths=widths),
        grid=(T // tm,),
        in_specs=[pl.BlockSpec((tm, D), lambda i: (i, 0)),
                  pl.BlockSpec((1, D), lambda i: (0, 0)),
                  pl.BlockSpec(w_main.shape, lambda i: (0, 0)),
                  pl.BlockSpec(w_if.shape, lambda i: (0, 0))],
        out_specs=out_specs,
        out_shape=out_shape,
        compiler_params=_params("parallel"),
        name="even_inproj",
    )(h, g, w_main, w_if)


def _sb_kernel(q_ref, k_ref, v_ref, o_ref, *, tq, tk):
    qi = pl.program_id(2)
    q2 = q_ref[0]
    lane = lax.broadcasted_iota(jnp.int32, (1, LANES), 1)
    first_head = lane < SB_HEAD_DIM
    rr = lax.broadcasted_iota(jnp.int32, (tk, tk), 0)
    cc = lax.broadcasted_iota(jnp.int32, (tk, tk), 1)
    cum_mat = jnp.where(rr > cc, 1.0, 0.0).astype(BF16)
    row_t = lax.broadcasted_iota(jnp.int32, (tq, tk), 0)
    col_s = lax.broadcasted_iota(jnp.int32, (tq, tk), 1)
    nsub = tq // tk
    heads = (first_head, jnp.logical_not(first_head))
    qhs = [jnp.where(keep, q2, jnp.zeros_like(q2)) for keep in heads]

    def weights(qh, ks, carry, diagonal):
        zn = lax.dot_general(qh, ks, (((1,), (1,)), ((), ())), preferred_element_type=F32)
        ps = [None] * nsub
        for sub in range(nsub - 1, -1, -1):
            zs = zn[:, sub * tk:(sub + 1) * tk]
            lk = jnp.minimum(zs, 0.0) - jnp.log2(1.0 + jnp.exp2(_neg_abs(zs)))
            if diagonal:
                causal = (col_s + sub * tk) < row_t
                lk = jnp.where(causal, lk, 0.0)
            lkb = lk.astype(BF16)
            cs = jnp.dot(lkb, cum_mat, preferred_element_type=F32)
            p = jnp.exp2(lk - zs + cs + carry)
            if diagonal:
                p = jnp.where(causal, p, 0.0)
            ps[sub] = p.astype(BF16)
            carry = carry + (cs[:, 0:1] + lkb[:, 0:1].astype(F32))
        return jnp.concatenate(ps, axis=1), carry

    def span(start, state, diagonal):
        c0, c1, acc = state
        ks = k_ref[0, pl.ds(start, tq), :]
        vs = v_ref[0, pl.ds(start, tq), :]
        p0, c0 = weights(qhs[0], ks, c0, diagonal)
        p1, c1 = weights(qhs[1], ks, c1, diagonal)
        v01 = jnp.concatenate([jnp.where(keep, vs, jnp.zeros_like(vs)) for keep in heads], axis=0)
        acc = acc + jnp.dot(jnp.concatenate([p0, p1], axis=1), v01, preferred_element_type=F32)
        return c0, c1, acc

    col0 = jnp.zeros((tq, 1), F32)
    zero = (col0, col0, jnp.zeros((tq, LANES), F32))
    start0 = pl.multiple_of(qi * tq, tq)

    def with_previous_span(_):
        st = span(start0, zero, True)
        return span(pl.multiple_of(start0 - tq, tq), st, False)

    def diagonal_only(_):
        return span(start0, zero, True)

    state = lax.cond(qi > 0, with_previous_span, diagonal_only, None)

    def mass_left(st):
        return (jnp.max(jnp.maximum(st[0], st[1])) > SB_LOG2_MASS_FLOOR).astype(jnp.int32)

    def cond(c):
        return (c[0] < qi) & (c[1] > 0)

    def body(c):
        j, _, st = c
        st = span(pl.multiple_of((qi - 1 - j) * tq, tq), st, False)
        return j + 1, mass_left(st), st

    _, _, state = lax.while_loop(cond, body, (jnp.int32(1), mass_left(state), state))
    o_ref[0] = state[2].astype(o_ref.dtype)


def _sb_attention(q, k, v):
    B, S, W = q.shape
    tq, tk = SB_Q_TILE, SB_K_TILE
    npair = W // LANES
    return pl.pallas_call(
        functools.partial(_sb_kernel, tq=tq, tk=tk),
        grid=(B, npair, S // tq),
        in_specs=[pl.BlockSpec((1, tq, LANES), lambda b, p, i: (b, i, p)),
                  pl.BlockSpec((1, S, LANES), lambda b, p, i: (b, 0, p)),
                  pl.BlockSpec((1, S, LANES), lambda b, p, i: (b, 0, p))],
        out_specs=pl.BlockSpec((1, tq, LANES), lambda b, p, i: (b, i, p)),
        out_shape=jax.ShapeDtypeStruct((B, S, W), BF16),
        compiler_params=_params("parallel", "parallel", "arbitrary"),
        name="sb_attention",
    )(q, k, v)


def _mlstm_kernel(q_ref, k_ref, v_ref, o_ref, gn_ref, gt_ref, bn_ref, bt_ref,
                  cwq_ref, cwk_ref, hg_ref, out_ref,
                  c_ref, m_ref, pq_ref, pk_ref, *, L):
    chunk = pl.program_id(1)

    @pl.when(chunk == 0)
    def _():
        c_ref[...] = jnp.zeros_like(c_ref)
        m_ref[...] = jnp.zeros_like(m_ref)
        pq_ref[...] = jnp.zeros_like(pq_ref)
        pk_ref[...] = jnp.zeros_like(pk_ref)

    def conv_silu(x, prev, w):
        xf = jnp.concatenate([prev, x], axis=0)
        y = xf * w[CONV_WIDTH - 1:CONV_WIDTH, :]
        for j in range(1, CONV_WIDTH):
            y = y + pltpu.roll(xf, j, 0) * w[CONV_WIDTH - 1 - j:CONV_WIDTH - j, :]
        y = y[HALO:, :]
        return y * _sigmoid(y)

    ri = lax.broadcasted_iota(jnp.int32, (L, L), 0)
    ci = lax.broadcasted_iota(jnp.int32, (L, L), 1)
    tril = ci <= ri
    tril_m = jnp.where(tril, 1.0, 0.0).astype(BF16)
    triu_m = jnp.where(ci >= ri, 1.0, 0.0).astype(BF16)
    gn = gn_ref[0] + bn_ref[...]
    gt = gt_ref[0] + bt_ref[...]
    ones = jnp.ones((L, LANES), BF16)

    def one_head(head):
        sl = slice(head * LANES, (head + 1) * LANES)
        q_raw = q_ref[0, :, sl].astype(F32)
        k_raw = k_ref[0, :, sl].astype(F32)
        q = conv_silu(q_raw, pq_ref[:, sl], cwq_ref[head])
        k = conv_silu(k_raw, pk_ref[:, sl], cwk_ref[head + ML_HEADS]) * (ML_HEAD_DIM ** -0.5)
        pq_ref[:, sl] = q_raw[L - HALO:, :]
        pk_ref[:, sl] = k_raw[L - HALO:, :]
        v = v_ref[0, :, sl]

        li_col = gn[:, head:head + 1]
        lf_col = -_softplus(-gn[:, head + ML_HEADS:head + ML_HEADS + 1])
        li_row = gt[head:head + 1, :]
        lf_row = -_softplus(-gt[head + ML_HEADS:head + ML_HEADS + 1, :])
        b_col = _dot_exact_x(tril_m, jnp.broadcast_to(lf_col, (L, LANES)))[:, :1]
        b_row = _dot_x_exact(jnp.broadcast_to(lf_row, (HALO, L)), triu_m)[:1, :]

        m_prev = m_ref[head, 0:1, 0:1]
        dmat = jnp.where(tril, b_col - b_row + li_row, -jnp.inf)
        inter = b_col + m_prev
        m_t = jnp.maximum(jnp.max(dmat, axis=1, keepdims=True), inter)
        scores = _dot_nt(q, k) * jnp.exp(dmat - m_t)
        w_inter = jnp.exp(inter - m_t)
        v_aug = jnp.concatenate([v, ones], axis=1)
        c_aug = c_ref[head]
        num_aug = _dot(scores, v_aug) + w_inter * _dot(q, c_aug)
        num = num_aug[:, :LANES]
        den = num_aug[:, LANES:LANES + 1]
        hval = num / jnp.maximum(jnp.abs(den), jnp.exp(-m_t))

        b_last = b_col[L - 1:L, :]
        gcol = b_last - b_col + li_col
        m_new = jnp.maximum(b_last + m_prev, jnp.max(gcol, axis=0, keepdims=True))
        w_state = jnp.exp(b_last + m_prev - m_new)
        w_tok = jnp.exp(gcol - m_new)
        c_ref[head] = w_state * c_aug + _dot_tn(k * w_tok, v_aug)
        m_ref[head] = jnp.broadcast_to(m_new, (HALO, LANES))

        hn = hval * lax.rsqrt(jnp.mean(hval * hval, axis=-1, keepdims=True) + NORM_EPS) * hg_ref[head]
        out_ref[0, :, sl] = (hn * _sigmoid(o_ref[0, :, sl].astype(F32))).astype(out_ref.dtype)

    for head in range(ML_HEADS):
        one_head(head)


def _mlstm(qk, v, o, gates_nat, gates_t, bias_nat, bias_t, conv_w, head_g):
    B, S, W = v.shape
    L = ML_CHUNK
    blk = lambda off: pl.BlockSpec((1, L, W), lambda b, c: (b, c, off))
    full = lambda arr: pl.BlockSpec(arr.shape, lambda b, c: (0,) * arr.ndim)
    return pl.pallas_call(
        functools.partial(_mlstm_kernel, L=L),
        grid=(B, S // L),
        in_specs=[blk(0), blk(1), blk(0), blk(0),
                  pl.BlockSpec((1, L, LANES), lambda b, c: (b, c, 0)),
                  pl.BlockSpec((1, 2 * ML_HEADS, L), lambda b, c: (b, 0, c)),
                  full(bias_nat), full(bias_t), full(conv_w), full(conv_w), full(head_g)],
        out_specs=pl.BlockSpec((1, L, W), lambda b, c: (b, c, 0)),
        out_shape=jax.ShapeDtypeStruct((B, S, W), BF16),
        scratch_shapes=[pltpu.VMEM((ML_HEADS, ML_HEAD_DIM, 2 * LANES), F32),
                        pltpu.VMEM((ML_HEADS, HALO, LANES), F32),
                        pltpu.VMEM((HALO, W), F32),
                        pltpu.VMEM((HALO, W), F32)],
        compiler_params=_params("parallel", "arbitrary"),
        name="mlstm",
    )(qk, qk, v, o, gates_nat, gates_t, bias_nat, bias_t, conv_w, conv_w, head_g)


def _mixmlp_kernel(*refs, nparts):
    parts = refs[:nparts]
    wo_ref, h_ref, g_ref, wu_ref, wd_ref, o_ref, u_ref = refs[nparts:]
    sub = MLP_ROW_SUB
    for s in range(h_ref.shape[0] // sub):
        rows = slice(s * sub, (s + 1) * sub)
        c0 = 0
        mix = None
        for p in parts:
            w = p.shape[1]
            t = jnp.dot(p[rows, :], wo_ref[c0:c0 + w, :], preferred_element_type=F32)
            mix = t if mix is None else mix + t
            c0 += w
        h1 = h_ref[rows, :] + _rms(mix, g_ref[1:2, :])
        o_ref[rows, :] = h1
        u_ref[rows, :] = _rms(h1, g_ref[2:3, :]).astype(BF16)

    u = u_ref[...]
    tf = MLP_FF_TILE
    acc = None
    for c in range(wu_ref.shape[1] // tf):
        a = jnp.maximum(jnp.dot(u, wu_ref[:, c * tf:(c + 1) * tf], preferred_element_type=F32), 0.0)
        t = jnp.dot((a * a).astype(BF16), wd_ref[c * tf:(c + 1) * tf, :], preferred_element_type=F32)
        acc = t if acc is None else acc + t
    o_ref[...] = o_ref[...] + _rms(acc, g_ref[3:4, :])


def _mix_mlp(parts, w_out, h, g, w_up, w_down):
    T, D = h.shape
    tm = MLP_ROW_TILE
    once = lambda arr: pl.BlockSpec(arr.shape, lambda i: (0, 0), pipeline_mode=pl.Buffered(1))
    in_specs = [pl.BlockSpec((tm, p.shape[1]), lambda i: (i, 0)) for p in parts]
    in_specs += [once(w_out), pl.BlockSpec((tm, D), lambda i: (i, 0)), once(g),
                 once(w_up), once(w_down)]
    return pl.pallas_call(
        functools.partial(_mixmlp_kernel, nparts=len(parts)),
        grid=(T // tm,),
        in_specs=in_specs,
        out_specs=pl.BlockSpec((tm, D), lambda i: (i, 0)),
        out_shape=jax.ShapeDtypeStruct((T, D), F32),
        scratch_shapes=[pltpu.VMEM((tm, D), BF16)],
        compiler_params=_params("parallel"),
        name="mix_mlp",
    )(*parts, w_out, h, g, w_up, w_down)


def _rwproj_kernel(x_ref, xh_ref, g_ref, mu_ref, wr_ref, wk_ref, wv_ref,
                   w1_ref, w2_ref, a1_ref, a2_ref, g1_ref, g2_ref, vec_ref,
                   r_ref, lw_ref, k_ref, v_ref, kk_ref, a_ref, gate_ref, *, tm, seq):
    i = pl.program_id(0)
    g = g_ref[...]
    w0, a0, k_k, k_a = (vec_ref[n:n + 1, :] for n in range(4))
    not_first = jnp.where((i * tm) % seq == 0, 0.0, 1.0)
    up = _rms(xh_ref[HALO - 1:HALO, :], g) * not_first
    u = _rms(x_ref[...], g)
    row = lax.broadcasted_iota(jnp.int32, (tm, 1), 0)
    x_prev = jnp.where(row == 0, up, pltpu.roll(u, 1, 0))
    xx = x_prev - u
    mix = lambda n: (u + xx * mu_ref[n:n + 1, :]).astype(BF16)

    r = jnp.dot(mix(0), wr_ref[...], preferred_element_type=F32)
    k = jnp.dot(mix(2), wk_ref[...], preferred_element_type=F32)
    v = jnp.dot(mix(3), wv_ref[...], preferred_element_type=F32)
    dw = _dot(jnp.tanh(jnp.dot(mix(1), w1_ref[...], preferred_element_type=F32)), w2_ref[...])
    log_w = -_softplus(-(w0 + dw)) - 0.5
    a = _sigmoid(a0 + _dot(jnp.dot(mix(4), a1_ref[...], preferred_element_type=F32), a2_ref[...]))
    gate = _dot(_sigmoid(jnp.dot(mix(5), g1_ref[...], preferred_element_type=F32)), g2_ref[...])

    r_ref[...] = r.astype(r_ref.dtype)
    lw_ref[...] = -jnp.exp(log_w)
    k_ref[...] = (k * (1.0 + (a - 1.0) * k_a)).astype(k_ref.dtype)
    v_ref[...] = v.astype(v_ref.dtype)
    kk_ref[...] = (k * k_k).astype(kk_ref.dtype)
    a_ref[...] = a.astype(a_ref.dtype)
    gate_ref[...] = gate.astype(gate_ref.dtype)


def _rwkv_proj(h, g, mu, wr, wk, wv, w1, w2, a1, a2, g1, g2, vecs, seq):
    T, D = h.shape
    tm = RW_ROW_TILE
    full = lambda arr: pl.BlockSpec(arr.shape, lambda i: (0, 0), pipeline_mode=pl.Buffered(1))
    row = pl.BlockSpec((tm, D), lambda i: (i, 0))
    halo = pl.BlockSpec((HALO, D), lambda i: (jnp.maximum(i * (tm // HALO) - 1, 0), 0))
    dtypes = (BF16, F32, BF16, BF16, BF16, BF16, BF16)
    return pl.pallas_call(
        functools.partial(_rwproj_kernel, tm=tm, seq=seq),
        grid=(T // tm,),
        in_specs=[row, halo, full(g), full(mu), full(wr), full(wk), full(wv),
                  full(w1), full(w2), full(a1), full(a2), full(g1), full(g2), full(vecs)],
        out_specs=[row] * 7,
        out_shape=[jax.ShapeDtypeStruct((T, D), dt) for dt in dtypes],
        compiler_params=_params("parallel"),
        name="rwkv_proj",
    )(h, h, g, mu, wr, wk, wv, w1, w2, a1, a2, g1, g2, vecs)


def _bmm(a, b):
    return lax.dot_general(a.astype(BF16), b.astype(BF16), (((2,), (1,)), ((0,), (0,))),
                           preferred_element_type=F32)


def _bmm_nt(a, b):
    return lax.dot_general(a.astype(BF16), b.astype(BF16), (((2,), (2,)), ((0,), (0,))),
                           preferred_element_type=F32)


def _bmm_tn(a, b):
    return lax.dot_general(a.astype(BF16), b.astype(BF16), (((1,), (1,)), ((0,), (0,))),
                           preferred_element_type=F32)


def _rwscan_kernel(r_ref, lw_ref, k_ref, v_ref, kk_ref, a_ref, gate_ref,
                   rk_ref, lng_ref, lnb_ref, o_ref, s_ref, *, L, nsub, npair):
    @pl.when(pl.program_id(2) == 0)
    def _():
        s_ref[...] = jnp.zeros_like(s_ref)

    P = 2 * L
    lane = lax.broadcasted_iota(jnp.int32, (1, 1, LANES), 2)
    ri = lax.broadcasted_iota(jnp.int32, (P, P), 0)
    ci = lax.broadcasted_iota(jnp.int32, (P, P), 1)
    same = (ri >> 6) == (ci >> 6)
    gsum = jnp.where(same, 1.0, 0.0).astype(BF16)
    strict = same & (ci < ri)
    incl = same & (ci <= ri)
    ri_l = lax.broadcasted_iota(jnp.int32, (L, L), 0)
    ci_l = lax.broadcasted_iota(jnp.int32, (L, L), 1)
    tril_l = jnp.where(ci_l <= ri_l, 1.0, 0.0).astype(BF16)

    nb = npair * nsub
    rows = nsub * L

    def pairs(x):
        return jnp.concatenate([x[:, p * LANES:(p + 1) * LANES] for p in range(npair)], axis=0)

    blk = lambda ref: pairs(ref[0].astype(F32)).reshape(nb, L, LANES)
    r, lw, k, v, kk, a = (blk(x) for x in (r_ref, lw_ref, k_ref, v_ref, kk_ref, a_ref))

    def lane_group_sum(x):
        return jnp.dot(x.astype(BF16), gsum, preferred_element_type=F32)

    ssq = lane_group_sum((kk * kk).reshape(nb * L, LANES)).reshape(nb, L, LANES)
    kap = kk * lax.rsqrt(jnp.maximum(ssq, 1e-24))
    bv = kap * a

    t1, t2 = _split(lw)
    tril_b = jnp.broadcast_to(tril_l, (nb, L, L))
    cum = _bmm(tril_b, t1) + _bmm(tril_b, t2)
    c_last = cum[:, L - 1:L, :]
    w_incl = jnp.exp(cum)
    w_excl = jnp.exp(cum - lw)
    w_inv = jnp.exp(-cum)
    w_rem = jnp.exp(c_last - cum)
    w_last = jnp.exp(c_last)

    bf = lambda x: x.astype(BF16)
    head0 = lane < RW_HEAD_DIM
    zb = jnp.zeros((), BF16)
    stack = lambda x: jnp.concatenate([jnp.where(head0, x, zb), jnp.where(head0, zb, x)], axis=1)
    dup = lambda x: jnp.concatenate([x, x], axis=1)
    unstack = lambda x: x[:, :L, :] + x[:, L:, :]

    r_dec = r * w_incl
    a_st = stack(bf(-kap * w_excl))
    r_st = stack(bf(r_dec))
    v_st = stack(bf(v))
    gram = _bmm_nt(jnp.concatenate([a_st, r_st], axis=1),
                   jnp.concatenate([dup(bf(bv * w_inv)), dup(bf(k * w_inv))], axis=1))
    a_ab = bf(jnp.where(strict, gram[:, :P, :P], 0.0))
    a_ak = bf(jnp.where(strict, gram[:, :P, P:], 0.0))
    a_rb = bf(jnp.where(incl, gram[:, P:, :P], 0.0))
    a_rk = bf(jnp.where(incl, gram[:, P:, P:], 0.0))

    tinv = jnp.where(ri == ci, jnp.ones((), BF16), jnp.where((ri >> 1) == (ci >> 1), a_ab, zb))
    n = 2
    while n < L:
        sh = n.bit_length() - 1
        off = ((ri >> (sh + 1)) == (ci >> (sh + 1))) & ((ri >> sh) != (ci >> sh))
        e = jnp.where(off, a_ab, zb)
        tinv = tinv + bf(_bmm(tinv, bf(_bmm(e, tinv))))
        n *= 2

    x1 = bf(_bmm(a_ak, v_st))
    tu = bf(_bmm(tinv, jnp.concatenate([x1, a_st], axis=2)))
    yr = _bmm(a_rb, tu)
    y0 = unstack(yr[:, :, :LANES] + _bmm(a_rk, v_st))
    rq = bf(r_dec + unstack(yr[:, :, LANES:]))
    bh_st = stack(bf(bv * w_rem))
    kh_st = stack(bf(k * w_rem))
    mn = _bmm_tn(bh_st, tu)
    m_bd = bf(mn[:, :, LANES:] + jnp.where(ri == ci, w_last, 0.0))
    n_bd = mn[:, :, :LANES] + _bmm_tn(kh_st, v_st)

    s = [s_ref[p] for p in range(npair)]
    ys = [[None] * nsub for _ in range(npair)]
    for c in range(nsub):
        for p in range(npair):
            i = p * nsub + c
            ys[p][c] = _dot(rq[i], s[p]) + y0[i]
            s[p] = _dot_exact_x(m_bd[i], s[p]) + n_bd[i]
    for p in range(npair):
        s_ref[p] = s[p]
    y = jnp.concatenate([jnp.concatenate(yp, axis=0) for yp in ys], axis=0)

    lanes = lambda x: jnp.concatenate([x[p * rows:(p + 1) * rows] for p in range(npair)], axis=1)
    inv_d = 1.0 / RW_HEAD_DIM
    mean = lane_group_sum(y) * inv_d
    d = y - mean
    var = lane_group_sum(d * d) * inv_d
    yn = lanes(d * lax.rsqrt(var + GN_EPS)) * lng_ref[...] + lnb_ref[...]
    r2, k2, v2 = (x[0].astype(F32) for x in (r_ref, k_ref, v_ref))
    bonus = lanes(lane_group_sum(pairs(r2 * k2 * rk_ref[...]))) * v2
    o_ref[0] = ((yn + bonus) * gate_ref[0].astype(F32)).astype(o_ref.dtype)


def _rwkv_scan(r, lw, k, v, kk, a, gate, r_k, ln_g, ln_b):
    B, S, D = r.shape
    L, lb, npair = RW_CHUNK, RW_BLOCK, RW_PAIRS
    width = npair * LANES
    blk = pl.BlockSpec((1, lb, width), lambda b, p, j: (b, j, p))
    vec = pl.BlockSpec((1, width), lambda b, p, j: (0, p))
    return pl.pallas_call(
        functools.partial(_rwscan_kernel, L=L, nsub=lb // L, npair=npair),
        grid=(B, D // width, S // lb),
        in_specs=[blk] * 7 + [vec] * 3,
        out_specs=blk,
        out_shape=jax.ShapeDtypeStruct((B, S, D), BF16),
        scratch_shapes=[pltpu.VMEM((npair, LANES, LANES), F32)],
        compiler_params=_params("parallel", "parallel", "arbitrary"),
        name="rwkv_scan",
    )(r, lw, k, v, kk, a, gate, r_k, ln_g, ln_b)


def _even_layer(h, B, S, g, w_in, b_if, conv_w, head_g, w_out, w_up, w_down):
    T, D = h.shape
    ncols = 3 * SB_WIDTH + 4 * ML_WIDTH
    w_main = w_in[:, :ncols].astype(BF16)
    w_if = jnp.pad(w_in[:, ncols:], ((0, 0), (0, LANES - 2 * ML_HEADS))).astype(BF16)
    sb_q, sb_k, sb_v, ml_qk, ml_v, ml_o, ml_if = _even_inproj(h, g[0:1], w_main, w_if)
    to3 = lambda t: t.reshape(B, S, t.shape[-1])
    a_out = _sb_attention(to3(sb_q), to3(sb_k), to3(sb_v))
    gates_nat = to3(ml_if)
    gates_t = jnp.swapaxes(gates_nat[:, :, :2 * ML_HEADS], 1, 2)
    bias_nat = jnp.pad(b_if, (0, LANES - 2 * ML_HEADS)).reshape(1, LANES)
    bias_t = jnp.broadcast_to(b_if[:, None], (2 * ML_HEADS, ML_CHUNK))
    cw = conv_w.reshape(CONV_WIDTH, 2 * ML_HEADS, LANES).transpose(1, 0, 2)
    hm = _mlstm(to3(ml_qk), to3(ml_v), to3(ml_o), gates_nat, gates_t, bias_nat, bias_t,
                cw, head_g.reshape(ML_HEADS, 1, LANES))
    return _mix_mlp([a_out.reshape(T, SB_WIDTH), hm.reshape(T, ML_WIDTH)],
                    w_out.astype(BF16), h, g, w_up.astype(BF16), w_down.astype(BF16))


def _odd_layer(h, B, S, g, mu, w_rkv, w0, w1, w2, a0, a1, a2, g1, g2, k_k, k_a, r_k,
               ln_g, ln_b, w_out, w_up, w_down):
    T, D = h.shape
    bf = lambda t: t.astype(BF16)
    mu8 = jnp.pad(mu, ((0, HALO - mu.shape[0]), (0, 0)))
    vecs = jnp.pad(jnp.stack([w0, a0, k_k, k_a]), ((0, HALO - 4), (0, 0)))
    r, lw, k, v, kk, a, gate = _rwkv_proj(
        h, g[0:1], mu8, bf(w_rkv[0]), bf(w_rkv[1]), bf(w_rkv[2]),
        bf(w1), bf(w2), bf(a1), bf(a2), bf(g1), bf(g2), vecs, S)
    to3 = lambda t: t.reshape(B, S, D)
    y = _rwkv_scan(to3(r), to3(lw), to3(k), to3(v), to3(kk), to3(a), to3(gate),
                   r_k.reshape(1, D), ln_g.reshape(1, D), ln_b.reshape(1, D))
    return _mix_mlp([y.reshape(T, D)], bf(w_out), h, g, bf(w_up), bf(w_down))


def kernel(x, norm_g, e_w_in, e_b_if, e_conv_w, e_head_g, e_w_out, r_mu, r_w_rkv, r_w0, r_w1, r_w2, r_a0, r_a1, r_a2, r_g1, r_g2, r_k_k, r_k_a, r_r_k, r_ln_g, r_ln_b, r_w_out, mlp_w_up, mlp_w_down):
    B, S, D = x.shape
    h = x.reshape(B * S, D)
    for layer in range(norm_g.shape[0]):
        g = norm_g[layer]
        if layer % 2 == 0:
            e = layer // 2
            h = _even_layer(h, B, S, g, e_w_in[e], e_b_if[e], e_conv_w[e], e_head_g[e],
                            e_w_out[e], mlp_w_up[layer], mlp_w_down[layer])
        else:
            o = layer // 2
            h = _odd_layer(h, B, S, g, r_mu[o], r_w_rkv[o], r_w0[o], r_w1[o], r_w2[o],
                           r_a0[o], r_a1[o], r_a2[o], r_g1[o], r_g2[o], r_k_k[o], r_k_a[o],
                           r_r_k[o], r_ln_g[o], r_ln_b[o], r_w_out[o],
                           mlp_w_up[layer], mlp_w_down[layer])
    return h.reshape(B, S, D)
```

```python
import functools
import math

import jax
import jax.numpy as jnp
from jax import lax
from jax.experimental import pallas as pl
from jax.experimental.pallas import tpu as pltpu

F32 = jnp.float32
BF16 = jnp.bfloat16

LANES = 128
V7X_VMEM_LIMIT_BYTES = 56 * 1024 * 1024

SB_HEADS = 8
SB_HEAD_DIM = 64
SB_WIDTH = SB_HEADS * SB_HEAD_DIM
ML_HEADS = 4
ML_HEAD_DIM = 128
ML_WIDTH = ML_HEADS * ML_HEAD_DIM
CONV_WIDTH = 4
RW_HEAD_DIM = 64
SB_Q_SCALE = -math.log2(math.e) / math.sqrt(SB_HEAD_DIM)
SB_LOG2_MASS_FLOOR = -160.0
NORM_EPS = 1e-6
GN_EPS = 64e-5

ROW_TILE = 512
RW_ROW_TILE = 512
MLP_ROW_TILE = 1024
MLP_ROW_SUB = 256
MLP_FF_TILE = 1024
SB_TILE = 256
ML_CHUNK = 256
RW_CHUNK = 64
RW_BLOCK = 256
RW_PAIRS = 8
HALO = 8


def _params(*sem):
    return pltpu.CompilerParams(dimension_semantics=sem,
                                vmem_limit_bytes=V7X_VMEM_LIMIT_BYTES)


def _dot(a, b):
    return jnp.dot(a.astype(BF16), b.astype(BF16), preferred_element_type=F32)


def _dot_nt(a, b):
    return lax.dot_general(a.astype(BF16), b.astype(BF16), (((1,), (1,)), ((), ())),
                           preferred_element_type=F32)


def _dot_tn(a, b):
    return lax.dot_general(a.astype(BF16), b.astype(BF16), (((0,), (0,)), ((), ())),
                           preferred_element_type=F32)


def _split(x):
    hi = x.astype(BF16)
    lo = (x - hi.astype(F32)).astype(BF16)
    return hi, lo


def _dot_x_exact(x, m):
    hi, lo = _split(x)
    return (jnp.dot(hi, m, preferred_element_type=F32)
            + jnp.dot(lo, m, preferred_element_type=F32))


def _dot_exact_x(m, x):
    hi, lo = _split(x)
    return (jnp.dot(m, hi, preferred_element_type=F32)
            + jnp.dot(m, lo, preferred_element_type=F32))


def _rms(x, g):
    ms = jnp.mean(x * x, axis=-1, keepdims=True)
    return x * lax.rsqrt(ms + NORM_EPS) * g


def _softplus(z):
    return jnp.maximum(z, 0.0) + jnp.log(1.0 + jnp.exp(-jnp.abs(z)))


def _neg_abs(x):
    bits = lax.bitcast_convert_type(x, jnp.uint32) | jnp.uint32(0x80000000)
    return lax.bitcast_convert_type(bits, F32)


def _sigmoid(z):
    return 1.0 / (1.0 + jnp.exp(-z))


def _inproj_kernel(x_ref, g_ref, w_ref, wif_ref, *out_refs, widths):
    u = _rms(x_ref[...], g_ref[...]).astype(BF16)
    c0 = 0
    for n, (o_ref, w) in enumerate(zip(out_refs[:-1], widths)):
        t = jnp.dot(u, w_ref[:, c0:c0 + w], preferred_element_type=F32)
        if n == 0:
            t = t * SB_Q_SCALE
        o_ref[...] = t.astype(o_ref.dtype)
        c0 += w
    out_refs[-1][...] = jnp.dot(u, wif_ref[...], preferred_element_type=F32)


def _even_inproj(h, g, w_main, w_if):
    T, D = h.shape
    widths = (SB_WIDTH, SB_WIDTH, SB_WIDTH, 2 * ML_WIDTH, ML_WIDTH, ML_WIDTH)
    tm = ROW_TILE
    out_shape = [jax.ShapeDtypeStruct((T, w), BF16) for w in widths]
    out_shape.append(jax.ShapeDtypeStruct((T, LANES), F32))
    out_specs = [pl.BlockSpec((tm, w), lambda i: (i, 0)) for w in widths]
    out_specs.append(pl.BlockSpec((tm, LANES), lambda i: (i, 0)))
    return pl.pallas_call(
        functools.partial(_inproj_kernel, widths=widths),
        grid=(T // tm,),
        in_specs=[pl.BlockSpec((tm, D), lambda i: (i, 0)),
                  pl.BlockSpec((1, D), lambda i: (0, 0)),
                  pl.BlockSpec(w_main.shape, lambda i: (0, 0)),
                  pl.BlockSpec(w_if.shape, lambda i: (0, 0))],
        out_specs=out_specs,
        out_shape=out_shape,
        compiler_params=_params("parallel"),
        name="even_inproj",
    )(h, g, w_main, w_if)


def _sb_kernel(q_ref, k_ref, v_ref, o_ref, *, tq):
    qi = pl.program_id(2)
    q2 = q_ref[0]
    lane = lax.broadcasted_iota(jnp.int32, (1, LANES), 1)
    first_head = lane < SB_HEAD_DIM
    rr = lax.broadcasted_iota(jnp.int32, (2 * tq, tq), 0)
    cc = lax.broadcasted_iota(jnp.int32, (2 * tq, tq), 1)
    lower = (rr & (tq - 1)) > cc
    cum_mat = jnp.where(lower[:tq], 1.0, 0.0).astype(BF16)
    heads = (first_head, jnp.logical_not(first_head))
    qst = jnp.concatenate([jnp.where(keep, q2, jnp.zeros_like(q2)) for keep in heads], axis=0)

    def span(start, state, diagonal):
        carry, acc = state
        ks = k_ref[0, pl.ds(start, tq), :]
        vs = v_ref[0, pl.ds(start, tq), :]
        zs = lax.dot_general(qst, ks, (((1,), (1,)), ((), ())), preferred_element_type=F32)
        lk = jnp.minimum(zs, 0.0) - jnp.log2(1.0 + jnp.exp2(_neg_abs(zs)))
        if diagonal:
            lk = jnp.where(lower, lk, 0.0)
        lkb = lk.astype(BF16)
        cs = jnp.dot(lkb, cum_mat, preferred_element_type=F32)
        p = jnp.exp2(lk - zs + cs + carry)
        if diagonal:
            p = jnp.where(lower, p, 0.0)
        p = p.astype(BF16)
        carry = carry + (cs[:, 0:1] + lkb[:, 0:1].astype(F32))
        v01 = jnp.concatenate([jnp.where(keep, vs, jnp.zeros_like(vs)) for keep in heads], axis=0)
        acc = acc + jnp.dot(jnp.concatenate([p[:tq], p[tq:]], axis=1), v01,
                            preferred_element_type=F32)
        return carry, acc

    zero = (jnp.zeros((2 * tq, 1), F32), jnp.zeros((tq, LANES), F32))
    start0 = pl.multiple_of(qi * tq, tq)

    def with_previous_span(_):
        st = span(start0, zero, True)
        return span(pl.multiple_of(start0 - tq, tq), st, False)

    def diagonal_only(_):
        return span(start0, zero, True)

    state = lax.cond(qi > 0, with_previous_span, diagonal_only, None)

    def mass_left(st):
        return (jnp.max(st[0]) > SB_LOG2_MASS_FLOOR).astype(jnp.int32)

    def cond(c):
        return (c[0] < qi) & (c[1] > 0)

    def body(c):
        j, _, st = c
        st = span(pl.multiple_of((qi - 1 - j) * tq, tq), st, False)
        return j + 1, mass_left(st), st

    _, _, state = lax.while_loop(cond, body, (jnp.int32(1), mass_left(state), state))
    o_ref[0] = state[1].astype(o_ref.dtype)


def _sb_attention(q, k, v):
    B, S, W = q.shape
    tq = SB_TILE
    npair = W // LANES
    return pl.pallas_call(
        functools.partial(_sb_kernel, tq=tq),
        grid=(B, npair, S // tq),
        in_specs=[pl.BlockSpec((1, tq, LANES), lambda b, p, i: (b, i, p)),
                  pl.BlockSpec((1, S, LANES), lambda b, p, i: (b, 0, p)),
                  pl.BlockSpec((1, S, LANES), lambda b, p, i: (b, 0, p))],
        out_specs=pl.BlockSpec((1, tq, LANES), lambda b, p, i: (b, i, p)),
        out_shape=jax.ShapeDtypeStruct((B, S, W), BF16),
        compiler_params=_params("parallel", "parallel", "arbitrary"),
        name="sb_attention",
    )(q, k, v)


def _mlstm_kernel(q_ref, k_ref, v_ref, o_ref, gn_ref, gt_ref, bn_ref, bt_ref,
                  cwq_ref, cwk_ref, hg_ref, out_ref,
                  c_ref, m_ref, pq_ref, pk_ref, *, L):
    chunk = pl.program_id(1)

    @pl.when(chunk == 0)
    def _():
        c_ref[...] = jnp.zeros_like(c_ref)
        m_ref[...] = jnp.zeros_like(m_ref)
        pq_ref[...] = jnp.zeros_like(pq_ref)
        pk_ref[...] = jnp.zeros_like(pk_ref)

    H = ML_HEADS
    heads = lambda x: jnp.stack([x[:, h * LANES:(h + 1) * LANES] for h in range(H)])
    unheads = lambda x: jnp.concatenate([x[h] for h in range(H)], axis=1)

    def conv_silu(x_ref, prev_ref, w):
        x = x_ref[0].astype(F32)
        xf = jnp.concatenate([prev_ref[...], x], axis=0)
        prev_ref[...] = x[L - HALO:, :]
        y = xf * w[CONV_WIDTH - 1:CONV_WIDTH, :]
        for j in range(1, CONV_WIDTH):
            y = y + pltpu.roll(xf, j, 0) * w[CONV_WIDTH - 1 - j:CONV_WIDTH - j, :]
        y = y[HALO:, :]
        return heads(y * _sigmoid(y))

    q = conv_silu(q_ref, pq_ref, cwq_ref[...])
    k = conv_silu(k_ref, pk_ref, cwk_ref[...]) * (ML_HEAD_DIM ** -0.5)
    v_aug = jnp.concatenate([heads(v_ref[0]), jnp.ones((H, L, LANES), BF16)], axis=2)

    ri = lax.broadcasted_iota(jnp.int32, (L, L), 0)
    ci = lax.broadcasted_iota(jnp.int32, (L, L), 1)
    tril = ci <= ri
    tril_m = jnp.where(tril, 1.0, 0.0).astype(BF16)
    triu_m = jnp.where(ci >= ri, 1.0, 0.0).astype(BF16)
    gn = gn_ref[0] + bn_ref[...]
    gt = gt_ref[0] + bt_ref[...]
    col = lambda x, n: jnp.stack([x[:, n + h:n + h + 1] for h in range(H)])
    row = lambda x, n: jnp.stack([x[n + h:n + h + 1, :] for h in range(H)])
    li_col = col(gn, 0)
    lf_col = -_softplus(-col(gn, H))
    li_row = row(gt, 0)
    lf_all = -_softplus(-gt)
    lf_wide = jnp.concatenate([jnp.broadcast_to(lf_col[h], (L, LANES)) for h in range(H)], axis=1)
    b_col = heads(_dot_exact_x(tril_m, lf_wide))[:, :, :1]
    b_row = row(_dot_x_exact(lf_all, triu_m), H)

    m_prev = m_ref[:, 0:1, 0:1]
    dmat = jnp.where(tril, b_col - b_row + li_row, -jnp.inf)
    inter = b_col + m_prev
    m_t = jnp.maximum(jnp.max(dmat, axis=2, keepdims=True), inter)
    scores = _bmm_nt(q, k) * jnp.exp(dmat - m_t)
    w_inter = jnp.exp(inter - m_t)
    c_aug = c_ref[...]
    num_aug = _bmm(scores, v_aug) + w_inter * _bmm(q, c_aug)
    num = num_aug[:, :, :LANES]
    den = num_aug[:, :, LANES:LANES + 1]
    hval = num / jnp.maximum(jnp.abs(den), jnp.exp(-m_t))

    b_last = b_col[:, L - 1:L, :]
    gcol = b_last - b_col + li_col
    m_new = jnp.maximum(b_last + m_prev, jnp.max(gcol, axis=1, keepdims=True))
    w_state = jnp.exp(b_last + m_prev - m_new)
    w_tok = jnp.exp(gcol - m_new)
    c_ref[...] = w_state * c_aug + _bmm_tn(k * w_tok, v_aug)
    m_ref[...] = jnp.broadcast_to(m_new, m_ref.shape)

    hn = hval * lax.rsqrt(jnp.mean(hval * hval, axis=-1, keepdims=True) + NORM_EPS) * hg_ref[...]
    out_ref[0] = (unheads(hn) * _sigmoid(o_ref[0].astype(F32))).astype(out_ref.dtype)


def _mlstm(qk, v, o, gates_nat, gates_t, bias_nat, bias_t, conv_w, head_g):
    B, S, W = v.shape
    L = ML_CHUNK
    blk = lambda off: pl.BlockSpec((1, L, W), lambda b, c: (b, c, off))
    full = lambda arr: pl.BlockSpec(arr.shape, lambda b, c: (0,) * arr.ndim)
    return pl.pallas_call(
        functools.partial(_mlstm_kernel, L=L),
        grid=(B, S // L),
        in_specs=[blk(0), blk(1), blk(0), blk(0),
                  pl.BlockSpec((1, L, LANES), lambda b, c: (b, c, 0)),
                  pl.BlockSpec((1, 2 * ML_HEADS, L), lambda b, c: (b, 0, c)),
                  full(bias_nat), full(bias_t),
                  pl.BlockSpec((CONV_WIDTH, W), lambda b, c: (0, 0)),
                  pl.BlockSpec((CONV_WIDTH, W), lambda b, c: (0, 1)),
                  full(head_g)],
        out_specs=pl.BlockSpec((1, L, W), lambda b, c: (b, c, 0)),
        out_shape=jax.ShapeDtypeStruct((B, S, W), BF16),
        scratch_shapes=[pltpu.VMEM((ML_HEADS, ML_HEAD_DIM, 2 * LANES), F32),
                        pltpu.VMEM((ML_HEADS, HALO, LANES), F32),
                        pltpu.VMEM((HALO, W), F32),
                        pltpu.VMEM((HALO, W), F32)],
        compiler_params=_params("parallel", "arbitrary"),
        name="mlstm",
    )(qk, qk, v, o, gates_nat, gates_t, bias_nat, bias_t, conv_w, conv_w, head_g)


def _mixmlp_kernel(*refs, nparts):
    parts = refs[:nparts]
    wo_ref, h_ref, g_ref, wu_ref, wd_ref, o_ref, u_ref = refs[nparts:]
    sub = MLP_ROW_SUB
    for s in range(h_ref.shape[0] // sub):
        rows = slice(s * sub, (s + 1) * sub)
        c0 = 0
        mix = None
        for p in parts:
            w = p.shape[1]
            t = jnp.dot(p[rows, :], wo_ref[c0:c0 + w, :], preferred_element_type=F32)
            mix = t if mix is None else mix + t
            c0 += w
        h1 = h_ref[rows, :] + _rms(mix, g_ref[1:2, :])
        o_ref[rows, :] = h1
        u_ref[rows, :] = _rms(h1, g_ref[2:3, :]).astype(BF16)

    u = u_ref[...]
    tf = MLP_FF_TILE
    acc = None
    for c in range(wu_ref.shape[1] // tf):
        a = jnp.maximum(jnp.dot(u, wu_ref[:, c * tf:(c + 1) * tf], preferred_element_type=F32), 0.0)
        t = jnp.dot((a * a).astype(BF16), wd_ref[c * tf:(c + 1) * tf, :], preferred_element_type=F32)
        acc = t if acc is None else acc + t
    o_ref[...] = o_ref[...] + _rms(acc, g_ref[3:4, :])


def _mix_mlp(parts, w_out, h, g, w_up, w_down):
    T, D = h.shape
    tm = MLP_ROW_TILE
    once = lambda arr: pl.BlockSpec(arr.shape, lambda i: (0, 0), pipeline_mode=pl.Buffered(1))
    in_specs = [pl.BlockSpec((tm, p.shape[1]), lambda i: (i, 0)) for p in parts]
    in_specs += [once(w_out), pl.BlockSpec((tm, D), lambda i: (i, 0)), once(g),
                 once(w_up), once(w_down)]
    return pl.pallas_call(
        functools.partial(_mixmlp_kernel, nparts=len(parts)),
        grid=(T // tm,),
        in_specs=in_specs,
        out_specs=pl.BlockSpec((tm, D), lambda i: (i, 0)),
        out_shape=jax.ShapeDtypeStruct((T, D), F32),
        scratch_shapes=[pltpu.VMEM((tm, D), BF16)],
        compiler_params=_params("parallel"),
        name="mix_mlp",
    )(*parts, w_out, h, g, w_up, w_down)


def _rwproj_kernel(x_ref, xh_ref, g_ref, mu_ref, wr_ref, wk_ref, wv_ref,
                   w1_ref, w2_ref, a1_ref, a2_ref, g1_ref, g2_ref, vec_ref,
                   r_ref, lw_ref, k_ref, v_ref, kk_ref, a_ref, gate_ref, *, tm, seq):
    i = pl.program_id(0)
    g = g_ref[...]
    w0, a0, k_k, k_a = (vec_ref[n:n + 1, :] for n in range(4))
    not_first = jnp.where((i * tm) % seq == 0, 0.0, 1.0)
    up = _rms(xh_ref[HALO - 1:HALO, :], g) * not_first
    u = _rms(x_ref[...], g)
    row = lax.broadcasted_iota(jnp.int32, (tm, 1), 0)
    x_prev = jnp.where(row == 0, up, pltpu.roll(u, 1, 0))
    ub = u.astype(BF16)
    xxb = (x_prev - u).astype(BF16)
    mub = mu_ref[...].astype(BF16)
    mix = lambda n: ub + xxb * mub[n:n + 1, :]

    r = jnp.dot(mix(0), wr_ref[...], preferred_element_type=F32)
    k = jnp.dot(mix(2), wk_ref[...], preferred_element_type=F32)
    v = jnp.dot(mix(3), wv_ref[...], preferred_element_type=F32)
    dw = _dot(jnp.tanh(jnp.dot(mix(1), w1_ref[...], preferred_element_type=F32)), w2_ref[...])
    a = _sigmoid(a0 + _dot(jnp.dot(mix(4), a1_ref[...], preferred_element_type=F32), a2_ref[...]))
    gate = _dot(_sigmoid(jnp.dot(mix(5), g1_ref[...], preferred_element_type=F32)), g2_ref[...])

    r_ref[...] = r.astype(r_ref.dtype)
    lw_ref[...] = (-math.exp(-0.5)) * _sigmoid(w0 + dw)
    k_ref[...] = (k * (1.0 + (a - 1.0) * k_a)).astype(k_ref.dtype)
    v_ref[...] = v.astype(v_ref.dtype)
    kk_ref[...] = (k * k_k).astype(kk_ref.dtype)
    a_ref[...] = a.astype(a_ref.dtype)
    gate_ref[...] = gate.astype(gate_ref.dtype)


def _rwkv_proj(h, g, mu, wr, wk, wv, w1, w2, a1, a2, g1, g2, vecs, seq):
    T, D = h.shape
    tm = RW_ROW_TILE
    full = lambda arr: pl.BlockSpec(arr.shape, lambda i: (0, 0), pipeline_mode=pl.Buffered(1))
    row = pl.BlockSpec((tm, D), lambda i: (i, 0))
    halo = pl.BlockSpec((HALO, D), lambda i: (jnp.maximum(i * (tm // HALO) - 1, 0), 0))
    dtypes = (BF16, F32, BF16, BF16, BF16, BF16, BF16)
    return pl.pallas_call(
        functools.partial(_rwproj_kernel, tm=tm, seq=seq),
        grid=(T // tm,),
        in_specs=[row, halo, full(g), full(mu), full(wr), full(wk), full(wv),
                  full(w1), full(w2), full(a1), full(a2), full(g1), full(g2), full(vecs)],
        out_specs=[row] * 7,
        out_shape=[jax.ShapeDtypeStruct((T, D), dt) for dt in dtypes],
        compiler_params=_params("parallel"),
        name="rwkv_proj",
    )(h, h, g, mu, wr, wk, wv, w1, w2, a1, a2, g1, g2, vecs)


def _bmm(a, b):
    return lax.dot_general(a.astype(BF16), b.astype(BF16), (((2,), (1,)), ((0,), (0,))),
                           preferred_element_type=F32)


def _bmm_nt(a, b):
    return lax.dot_general(a.astype(BF16), b.astype(BF16), (((2,), (2,)), ((0,), (0,))),
                           preferred_element_type=F32)


def _bmm_tn(a, b):
    return lax.dot_general(a.astype(BF16), b.astype(BF16), (((1,), (1,)), ((0,), (0,))),
                           preferred_element_type=F32)


def _rwscan_kernel(r_ref, lw_ref, k_ref, v_ref, kk_ref, a_ref, gate_ref,
                   rk_ref, lng_ref, lnb_ref, o_ref, s_ref, *, L, nsub, npair):
    @pl.when(pl.program_id(2) == 0)
    def _():
        s_ref[...] = jnp.zeros_like(s_ref)

    P = 2 * L
    lane = lax.broadcasted_iota(jnp.int32, (1, 1, LANES), 2)
    ri = lax.broadcasted_iota(jnp.int32, (P, P), 0)
    ci = lax.broadcasted_iota(jnp.int32, (P, P), 1)
    same = (ri >> 6) == (ci >> 6)
    gsum = jnp.where(same, 1.0, 0.0).astype(BF16)
    strict = same & (ci < ri)
    incl = same & (ci <= ri)
    ri_l = lax.broadcasted_iota(jnp.int32, (L, L), 0)
    ci_l = lax.broadcasted_iota(jnp.int32, (L, L), 1)
    tril_l = jnp.where(ci_l <= ri_l, 1.0, 0.0).astype(BF16)

    nb = npair * nsub
    rows = nsub * L

    def pairs(x):
        return jnp.concatenate([x[:, p * LANES:(p + 1) * LANES] for p in range(npair)], axis=0)

    blk = lambda ref: pairs(ref[0].astype(F32)).reshape(nb, L, LANES)
    r, lw, k, v, kk, a = (blk(x) for x in (r_ref, lw_ref, k_ref, v_ref, kk_ref, a_ref))

    def lane_group_sum(x):
        return jnp.dot(x.astype(BF16), gsum, preferred_element_type=F32)

    ssq = lane_group_sum((kk * kk).reshape(nb * L, LANES)).reshape(nb, L, LANES)
    kap = kk * lax.rsqrt(jnp.maximum(ssq, 1e-24))
    bv = kap * a

    t1, t2 = _split(lw)
    tril_b = jnp.broadcast_to(tril_l, (nb, L, L))
    cum = _bmm(tril_b, t1) + _bmm(tril_b, t2)
    c_last = cum[:, L - 1:L, :]
    w_incl = jnp.exp(cum)
    w_excl = jnp.exp(cum - lw)
    w_inv = jnp.exp(-cum)
    w_rem = jnp.exp(c_last - cum)
    w_last = jnp.exp(c_last)

    bf = lambda x: x.astype(BF16)
    head0 = lane < RW_HEAD_DIM
    zb = jnp.zeros((), BF16)
    stack = lambda x: jnp.concatenate([jnp.where(head0, x, zb), jnp.where(head0, zb, x)], axis=1)
    dup = lambda x: jnp.concatenate([x, x], axis=1)
    unstack = lambda x: x[:, :L, :] + x[:, L:, :]

    r_dec = r * w_incl
    a_st = stack(bf(-kap * w_excl))
    r_st = stack(bf(r_dec))
    v_st = stack(bf(v))
    gram = _bmm_nt(jnp.concatenate([a_st, r_st], axis=1),
                   jnp.concatenate([dup(bf(bv * w_inv)), dup(bf(k * w_inv))], axis=1))
    a_ab = bf(jnp.where(strict, gram[:, :P, :P], 0.0))
    a_ak = bf(jnp.where(strict, gram[:, :P, P:], 0.0))
    a_rb = bf(jnp.where(incl, gram[:, P:, :P], 0.0))
    a_rk = bf(jnp.where(incl, gram[:, P:, P:], 0.0))

    tinv = jnp.where(ri == ci, jnp.ones((), BF16), jnp.where((ri >> 1) == (ci >> 1), a_ab, zb))
    n = 2
    while n < L:
        sh = n.bit_length() - 1
        off = ((ri >> (sh + 1)) == (ci >> (sh + 1))) & ((ri >> sh) != (ci >> sh))
        e = jnp.where(off, a_ab, zb)
        tinv = tinv + bf(_bmm(tinv, bf(_bmm(e, tinv))))
        n *= 2

    x1 = bf(_bmm(a_ak, v_st))
    tu = bf(_bmm(tinv, jnp.concatenate([x1, a_st], axis=2)))
    yr = _bmm(a_rb, tu)
    y0 = unstack(yr[:, :, :LANES] + _bmm(a_rk, v_st))
    rq = bf(r_dec + unstack(yr[:, :, LANES:]))
    bh_st = stack(bf(bv * w_rem))
    kh_st = stack(bf(k * w_rem))
    mn = _bmm_tn(bh_st, tu)
    m_bd = bf(mn[:, :, LANES:] + jnp.where(ri == ci, w_last, 0.0))
    n_bd = mn[:, :, :LANES] + _bmm_tn(kh_st, v_st)

    s = [s_ref[p] for p in range(npair)]
    ys = [[None] * nsub for _ in range(npair)]
    for c in range(nsub):
        for p in range(npair):
            i = p * nsub + c
            ys[p][c] = _dot(rq[i], s[p]) + y0[i]
            s[p] = _dot_exact_x(m_bd[i], s[p]) + n_bd[i]
    for p in range(npair):
        s_ref[p] = s[p]
    y = jnp.concatenate([jnp.concatenate(yp, axis=0) for yp in ys], axis=0)

    lanes = lambda x: jnp.concatenate([x[p * rows:(p + 1) * rows] for p in range(npair)], axis=1)
    inv_d = 1.0 / RW_HEAD_DIM
    mean = lane_group_sum(y) * inv_d
    d = y - mean
    var = lane_group_sum(d * d) * inv_d
    yn = lanes(d * lax.rsqrt(var + GN_EPS)) * lng_ref[...] + lnb_ref[...]
    r2, k2, v2 = (x[0].astype(F32) for x in (r_ref, k_ref, v_ref))
    bonus = lanes(lane_group_sum(pairs(r2 * k2 * rk_ref[...]))) * v2
    o_ref[0] = ((yn + bonus) * gate_ref[0].astype(F32)).astype(o_ref.dtype)


def _rwkv_scan(r, lw, k, v, kk, a, gate, r_k, ln_g, ln_b):
    B, S, D = r.shape
    L, lb, npair = RW_CHUNK, RW_BLOCK, RW_PAIRS
    width = npair * LANES
    blk = pl.BlockSpec((1, lb, width), lambda b, p, j: (b, j, p))
    vec = pl.BlockSpec((1, width), lambda b, p, j: (0, p))
    return pl.pallas_call(
        functools.partial(_rwscan_kernel, L=L, nsub=lb // L, npair=npair),
        grid=(B, D // width, S // lb),
        in_specs=[blk] * 7 + [vec] * 3,
        out_specs=blk,
        out_shape=jax.ShapeDtypeStruct((B, S, D), BF16),
        scratch_shapes=[pltpu.VMEM((npair, LANES, LANES), F32)],
        compiler_params=_params("parallel", "parallel", "arbitrary"),
        name="rwkv_scan",
    )(r, lw, k, v, kk, a, gate, r_k, ln_g, ln_b)


def _even_layer(h, B, S, g, w_in, b_if, conv_w, head_g, w_out, w_up, w_down):
    T, D = h.shape
    ncols = 3 * SB_WIDTH + 4 * ML_WIDTH
    w_main = w_in[:, :ncols].astype(BF16)
    w_if = jnp.pad(w_in[:, ncols:], ((0, 0), (0, LANES - 2 * ML_HEADS))).astype(BF16)
    sb_q, sb_k, sb_v, ml_qk, ml_v, ml_o, ml_if = _even_inproj(h, g[0:1], w_main, w_if)
    to3 = lambda t: t.reshape(B, S, t.shape[-1])
    a_out = _sb_attention(to3(sb_q), to3(sb_k), to3(sb_v))
    gates_nat = to3(ml_if)
    gates_t = jnp.swapaxes(gates_nat[:, :, :2 * ML_HEADS], 1, 2)
    bias_nat = jnp.pad(b_if, (0, LANES - 2 * ML_HEADS)).reshape(1, LANES)
    bias_t = jnp.broadcast_to(b_if[:, None], (2 * ML_HEADS, ML_CHUNK))
    hm = _mlstm(to3(ml_qk), to3(ml_v), to3(ml_o), gates_nat, gates_t, bias_nat, bias_t,
                conv_w, head_g.reshape(ML_HEADS, 1, LANES))
    return _mix_mlp([a_out.reshape(T, SB_WIDTH), hm.reshape(T, ML_WIDTH)],
                    w_out.astype(BF16), h, g, w_up.astype(BF16), w_down.astype(BF16))


def _odd_layer(h, B, S, g, mu, w_rkv, w0, w1, w2, a0, a1, a2, g1, g2, k_k, k_a, r_k,
               ln_g, ln_b, w_out, w_up, w_down):
    T, D = h.shape
    bf = lambda t: t.astype(BF16)
    mu8 = jnp.pad(mu, ((0, HALO - mu.shape[0]), (0, 0)))
    vecs = jnp.pad(jnp.stack([w0, a0, k_k, k_a]), ((0, HALO - 4), (0, 0)))
    r, lw, k, v, kk, a, gate = _rwkv_proj(
        h, g[0:1], mu8, bf(w_rkv[0]), bf(w_rkv[1]), bf(w_rkv[2]),
        bf(w1), bf(w2), bf(a1), bf(a2), bf(g1), bf(g2), vecs, S)
    to3 = lambda t: t.reshape(B, S, D)
    y = _rwkv_scan(to3(r), to3(lw), to3(k), to3(v), to3(kk), to3(a), to3(gate),
                   r_k.reshape(1, D), ln_g.reshape(1, D), ln_b.reshape(1, D))
    return _mix_mlp([y.reshape(T, D)], bf(w_out), h, g, bf(w_up), bf(w_down))


def kernel(x, norm_g, e_w_in, e_b_if, e_conv_w, e_head_g, e_w_out, r_mu, r_w_rkv, r_w0, r_w1, r_w2, r_a0, r_a1, r_a2, r_g1, r_g2, r_k_k, r_k_a, r_r_k, r_ln_g, r_ln_b, r_w_out, mlp_w_up, mlp_w_down):
    B, S, D = x.shape
    h = x.reshape(B * S, D)
    for layer in range(norm_g.shape[0]):
        g = norm_g[layer]
        if layer % 2 == 0:
            e = layer // 2
            h = _even_layer(h, B, S, g, e_w_in[e], e_b_if[e], e_conv_w[e], e_head_g[e],
                            e_w_out[e], mlp_w_up[layer], mlp_w_down[layer])
        else:
            o = layer // 2
            h = _odd_layer(h, B, S, g, r_mu[o], r_w_rkv[o], r_w0[o], r_w1[o], r_w2[o],
                           r_a0[o], r_a1[o], r_a2[o], r_g1[o], r_g2[o], r_k_k[o], r_k_a[o],
                           r_r_k[o], r_ln_g[o], r_ln_b[o], r_w_out[o],
                           mlp_w_up[layer], mlp_w_down[layer])
    return h.reshape(B, S, D)
```

```python
import functools
import math

import jax
import jax.numpy as jnp
from jax import lax
from jax.experimental import pallas as pl
from jax.experimental.pallas import tpu as pltpu

F32 = jnp.float32
BF16 = jnp.bfloat16

LANES = 128
V7X_VMEM_LIMIT_BYTES = 56 * 1024 * 1024

SB_HEADS = 8
SB_HEAD_DIM = 64
SB_WIDTH = SB_HEADS * SB_HEAD_DIM
ML_HEADS = 4
ML_HEAD_DIM = 128
ML_WIDTH = ML_HEADS * ML_HEAD_DIM
CONV_WIDTH = 4
RW_HEAD_DIM = 64
SB_Q_SCALE = -math.log2(math.e) / math.sqrt(SB_HEAD_DIM)
SB_LOG2_MASS_FLOOR = -160.0
NORM_EPS = 1e-6
GN_EPS = 64e-5

ROW_TILE = 512
RW_ROW_TILE = 512
MLP_ROW_TILE = 1024
MLP_ROW_SUB = 256
MLP_FF_TILE = 1024
SB_TILE = 256
SB_TILES_PER_STEP = 4
ML_CHUNK = 256
RW_CHUNK = 64
RW_BLOCK = 256
RW_PAIRS = 8
HALO = 8


def _params(*sem):
    return pltpu.CompilerParams(dimension_semantics=sem,
                                vmem_limit_bytes=V7X_VMEM_LIMIT_BYTES)


def _dot(a, b):
    return jnp.dot(a.astype(BF16), b.astype(BF16), preferred_element_type=F32)


def _dot_nt(a, b):
    return lax.dot_general(a.astype(BF16), b.astype(BF16), (((1,), (1,)), ((), ())),
                           preferred_element_type=F32)


def _dot_tn(a, b):
    return lax.dot_general(a.astype(BF16), b.astype(BF16), (((0,), (0,)), ((), ())),
                           preferred_element_type=F32)


def _split(x):
    hi = x.astype(BF16)
    lo = (x - hi.astype(F32)).astype(BF16)
    return hi, lo


def _dot_x_exact(x, m):
    hi, lo = _split(x)
    return (jnp.dot(hi, m, preferred_element_type=F32)
            + jnp.dot(lo, m, preferred_element_type=F32))


def _dot_exact_x(m, x):
    hi, lo = _split(x)
    return (jnp.dot(m, hi, preferred_element_type=F32)
            + jnp.dot(m, lo, preferred_element_type=F32))


def _rms(x, g):
    ms = jnp.mean(x * x, axis=-1, keepdims=True)
    return x * lax.rsqrt(ms + NORM_EPS) * g


def _softplus(z):
    return jnp.maximum(z, 0.0) + jnp.log(1.0 + jnp.exp(-jnp.abs(z)))


def _neg_abs(x):
    bits = lax.bitcast_convert_type(x, jnp.uint32) | jnp.uint32(0x80000000)
    return lax.bitcast_convert_type(bits, F32)


def _sigmoid(z):
    return 1.0 / (1.0 + jnp.exp(-z))


def _inproj_kernel(x_ref, g_ref, w_ref, wif_ref, *out_refs, widths):
    u = _rms(x_ref[...], g_ref[...]).astype(BF16)
    c0 = 0
    for n, (o_ref, w) in enumerate(zip(out_refs[:-1], widths)):
        t = jnp.dot(u, w_ref[:, c0:c0 + w], preferred_element_type=F32)
        if n == 0:
            t = t * SB_Q_SCALE
        o_ref[...] = t.astype(o_ref.dtype)
        c0 += w
    out_refs[-1][...] = jnp.dot(u, wif_ref[...], preferred_element_type=F32)


def _even_inproj(h, g, w_main, w_if):
    T, D = h.shape
    widths = (SB_WIDTH, SB_WIDTH, SB_WIDTH, 2 * ML_WIDTH, ML_WIDTH, ML_WIDTH)
    tm = ROW_TILE
    out_shape = [jax.ShapeDtypeStruct((T, w), BF16) for w in widths]
    out_shape.append(jax.ShapeDtypeStruct((T, LANES), F32))
    out_specs = [pl.BlockSpec((tm, w), lambda i: (i, 0)) for w in widths]
    out_specs.append(pl.BlockSpec((tm, LANES), lambda i: (i, 0)))
    return pl.pallas_call(
        functools.partial(_inproj_kernel, widths=widths),
        grid=(T // tm,),
        in_specs=[pl.BlockSpec((tm, D), lambda i: (i, 0)),
                  pl.BlockSpec((1, D), lambda i: (0, 0)),
                  pl.BlockSpec(w_main.shape, lambda i: (0, 0)),
                  pl.BlockSpec(w_if.shape, lambda i: (0, 0))],
        out_specs=out_specs,
        out_shape=out_shape,
        compiler_params=_params("parallel"),
        name="even_inproj",
    )(h, g, w_main, w_if)


def _sb_kernel(q_ref, k_ref, v_ref, o_ref, *, tq, ntile):
    lane = lax.broadcasted_iota(jnp.int32, (1, LANES), 1)
    first_head = lane < SB_HEAD_DIM
    rr = lax.broadcasted_iota(jnp.int32, (tq, tq), 0)
    cc = lax.broadcasted_iota(jnp.int32, (tq, tq), 1)
    lower = rr > cc
    cum_mat = jnp.where(lower, 1.0, 0.0).astype(BF16)
    heads = (first_head, jnp.logical_not(first_head))

    def weights(qh, ks, carry, diagonal):
        zs = lax.dot_general(qh, ks, (((1,), (1,)), ((), ())), preferred_element_type=F32)
        lk = jnp.minimum(zs, 0.0) - jnp.log2(1.0 + jnp.exp2(_neg_abs(zs)))
        if diagonal:
            lk = jnp.where(lower, lk, 0.0)
        lkb = lk.astype(BF16)
        cs = jnp.dot(lkb, cum_mat, preferred_element_type=F32)
        p = jnp.exp2(lk - zs + cs + carry)
        if diagonal:
            p = jnp.where(lower, p, 0.0)
        return p.astype(BF16), carry + (cs[:, 0:1] + lkb[:, 0:1].astype(F32))

    def span(qhs, start, state, diagonal):
        c0, c1, acc = state
        ks = k_ref[0, pl.ds(start, tq), :]
        vs = v_ref[0, pl.ds(start, tq), :]
        p0, c0 = weights(qhs[0], ks, c0, diagonal)
        p1, c1 = weights(qhs[1], ks, c1, diagonal)
        v01 = jnp.concatenate([jnp.where(keep, vs, jnp.zeros_like(vs)) for keep in heads], axis=0)
        acc = acc + jnp.dot(jnp.concatenate([p0, p1], axis=1), v01, preferred_element_type=F32)
        return c0, c1, acc

    col0 = jnp.zeros((tq, 1), F32)
    zero = (col0, col0, jnp.zeros((tq, LANES), F32))

    def mass_left(st):
        return (jnp.max(jnp.maximum(st[0], st[1])) > SB_LOG2_MASS_FLOOR).astype(jnp.int32)

    for t in range(ntile):
        qi = pl.program_id(2) * ntile + t
        rows = slice(t * tq, (t + 1) * tq)
        q2 = q_ref[0, rows, :]
        qhs = [jnp.where(keep, q2, jnp.zeros_like(q2)) for keep in heads]
        start0 = pl.multiple_of(qi * tq, tq)

        def with_previous_span(_):
            st = span(qhs, start0, zero, True)
            return span(qhs, pl.multiple_of(start0 - tq, tq), st, False)

        def diagonal_only(_):
            return span(qhs, start0, zero, True)

        if t == 0:
            state = lax.cond(qi > 0, with_previous_span, diagonal_only, None)
        else:
            state = with_previous_span(None)

        def cond(c):
            return (c[0] < qi) & (c[1] > 0)

        def body(c):
            j, _, st = c
            st = span(qhs, pl.multiple_of((qi - 1 - j) * tq, tq), st, False)
            return j + 1, mass_left(st), st

        _, _, state = lax.while_loop(cond, body, (jnp.int32(1), mass_left(state), state))
        o_ref[0, rows, :] = state[2].astype(o_ref.dtype)


def _sb_attention(q, k, v):
    B, S, W = q.shape
    tq, ntile = SB_TILE, SB_TILES_PER_STEP
    npair = W // LANES
    rows = tq * ntile
    return pl.pallas_call(
        functools.partial(_sb_kernel, tq=tq, ntile=ntile),
        grid=(B, npair, S // rows),
        in_specs=[pl.BlockSpec((1, rows, LANES), lambda b, p, i: (b, i, p)),
                  pl.BlockSpec((1, S, LANES), lambda b, p, i: (b, 0, p)),
                  pl.BlockSpec((1, S, LANES), lambda b, p, i: (b, 0, p))],
        out_specs=pl.BlockSpec((1, rows, LANES), lambda b, p, i: (b, i, p)),
        out_shape=jax.ShapeDtypeStruct((B, S, W), BF16),
        compiler_params=_params("parallel", "parallel", "arbitrary"),
        name="sb_attention",
    )(q, k, v)


def _mlstm_kernel(q_ref, k_ref, v_ref, o_ref, gn_ref, gt_ref, bn_ref, bt_ref,
                  cwq_ref, cwk_ref, hg_ref, out_ref,
                  c_ref, m_ref, pq_ref, pk_ref, *, L):
    chunk = pl.program_id(1)

    @pl.when(chunk == 0)
    def _():
        c_ref[...] = jnp.zeros_like(c_ref)
        m_ref[...] = jnp.zeros_like(m_ref)
        pq_ref[...] = jnp.zeros_like(pq_ref)
        pk_ref[...] = jnp.zeros_like(pk_ref)

    H = ML_HEADS
    heads = lambda x: jnp.stack([x[:, h * LANES:(h + 1) * LANES] for h in range(H)])
    unheads = lambda x: jnp.concatenate([x[h] for h in range(H)], axis=1)

    def conv_silu(x_ref, prev_ref, w):
        x = x_ref[0].astype(F32)
        xf = jnp.concatenate([prev_ref[...], x], axis=0)
        prev_ref[...] = x[L - HALO:, :]
        y = xf * w[CONV_WIDTH - 1:CONV_WIDTH, :]
        for j in range(1, CONV_WIDTH):
            y = y + pltpu.roll(xf, j, 0) * w[CONV_WIDTH - 1 - j:CONV_WIDTH - j, :]
        y = y[HALO:, :]
        return heads(y * _sigmoid(y))

    q = conv_silu(q_ref, pq_ref, cwq_ref[...])
    k = conv_silu(k_ref, pk_ref, cwk_ref[...]) * (ML_HEAD_DIM ** -0.5)
    v_aug = jnp.concatenate([heads(v_ref[0]), jnp.ones((H, L, LANES), BF16)], axis=2)

    ri = lax.broadcasted_iota(jnp.int32, (L, L), 0)
    ci = lax.broadcasted_iota(jnp.int32, (L, L), 1)
    tril = ci <= ri
    tril_m = jnp.where(tril, 1.0, 0.0).astype(BF16)
    triu_m = jnp.where(ci >= ri, 1.0, 0.0).astype(BF16)
    gn = gn_ref[0] + bn_ref[...]
    gt = gt_ref[0] + bt_ref[...]
    col = lambda x, n: jnp.stack([x[:, n + h:n + h + 1] for h in range(H)])
    row = lambda x, n: jnp.stack([x[n + h:n + h + 1, :] for h in range(H)])
    li_col = col(gn, 0)
    lf_col = -_softplus(-col(gn, H))
    li_row = row(gt, 0)
    lf_all = -_softplus(-gt)
    lf_wide = jnp.concatenate([jnp.broadcast_to(lf_col[h], (L, LANES)) for h in range(H)], axis=1)
    b_col = heads(_dot_exact_x(tril_m, lf_wide))[:, :, :1]
    b_row = row(_dot_x_exact(lf_all, triu_m), H)

    m_prev = m_ref[:, 0:1, 0:1]
    dmat = jnp.where(tril, b_col - b_row + li_row, -jnp.inf)
    inter = b_col + m_prev
    m_t = jnp.maximum(jnp.max(dmat, axis=2, keepdims=True), inter)
    scores = _bmm_nt(q, k) * jnp.exp(dmat - m_t)
    w_inter = jnp.exp(inter - m_t)
    c_aug = c_ref[...]
    num_aug = _bmm(scores, v_aug) + w_inter * _bmm(q, c_aug)
    num = num_aug[:, :, :LANES]
    den = num_aug[:, :, LANES:LANES + 1]
    hval = num / jnp.maximum(jnp.abs(den), jnp.exp(-m_t))

    b_last = b_col[:, L - 1:L, :]
    gcol = b_last - b_col + li_col
    m_new = jnp.maximum(b_last + m_prev, jnp.max(gcol, axis=1, keepdims=True))
    w_state = jnp.exp(b_last + m_prev - m_new)
    w_tok = jnp.exp(gcol - m_new)
    c_ref[...] = w_state * c_aug + _bmm_tn(k * w_tok, v_aug)
    m_ref[...] = jnp.broadcast_to(m_new, m_ref.shape)

    hn = hval * lax.rsqrt(jnp.mean(hval * hval, axis=-1, keepdims=True) + NORM_EPS) * hg_ref[...]
    out_ref[0] = (unheads(hn) * _sigmoid(o_ref[0].astype(F32))).astype(out_ref.dtype)


def _mlstm(qk, v, o, gates_nat, gates_t, bias_nat, bias_t, conv_w, head_g):
    B, S, W = v.shape
    L = ML_CHUNK
    blk = lambda off: pl.BlockSpec((1, L, W), lambda b, c: (b, c, off))
    full = lambda arr: pl.BlockSpec(arr.shape, lambda b, c: (0,) * arr.ndim)
    return pl.pallas_call(
        functools.partial(_mlstm_kernel, L=L),
        grid=(B, S // L),
        in_specs=[blk(0), blk(1), blk(0), blk(0),
                  pl.BlockSpec((1, L, LANES), lambda b, c: (b, c, 0)),
                  pl.BlockSpec((1, 2 * ML_HEADS, L), lambda b, c: (b, 0, c)),
                  full(bias_nat), full(bias_t),
                  pl.BlockSpec((CONV_WIDTH, W), lambda b, c: (0, 0)),
                  pl.BlockSpec((CONV_WIDTH, W), lambda b, c: (0, 1)),
                  full(head_g)],
        out_specs=pl.BlockSpec((1, L, W), lambda b, c: (b, c, 0)),
        out_shape=jax.ShapeDtypeStruct((B, S, W), BF16),
        scratch_shapes=[pltpu.VMEM((ML_HEADS, ML_HEAD_DIM, 2 * LANES), F32),
                        pltpu.VMEM((ML_HEADS, HALO, LANES), F32),
                        pltpu.VMEM((HALO, W), F32),
                        pltpu.VMEM((HALO, W), F32)],
        compiler_params=_params("parallel", "arbitrary"),
        name="mlstm",
    )(qk, qk, v, o, gates_nat, gates_t, bias_nat, bias_t, conv_w, conv_w, head_g)


def _mixmlp_kernel(*refs, nparts):
    parts = refs[:nparts]
    wo_ref, h_ref, g_ref, wu_ref, wd_ref, o_ref, u_ref = refs[nparts:]
    sub = MLP_ROW_SUB
    for s in range(h_ref.shape[0] // sub):
        rows = slice(s * sub, (s + 1) * sub)
        c0 = 0
        mix = None
        for p in parts:
            w = p.shape[1]
            t = jnp.dot(p[rows, :], wo_ref[c0:c0 + w, :], preferred_element_type=F32)
            mix = t if mix is None else mix + t
            c0 += w
        h1 = h_ref[rows, :] + _rms(mix, g_ref[1:2, :])
        o_ref[rows, :] = h1
        u_ref[rows, :] = _rms(h1, g_ref[2:3, :]).astype(BF16)

    u = u_ref[...]
    tf = MLP_FF_TILE
    acc = None
    for c in range(wu_ref.shape[1] // tf):
        a = jnp.maximum(jnp.dot(u, wu_ref[:, c * tf:(c + 1) * tf], preferred_element_type=F32), 0.0)
        t = jnp.dot((a * a).astype(BF16), wd_ref[c * tf:(c + 1) * tf, :], preferred_element_type=F32)
        acc = t if acc is None else acc + t
    o_ref[...] = o_ref[...] + _rms(acc, g_ref[3:4, :])


def _mix_mlp(parts, w_out, h, g, w_up, w_down):
    T, D = h.shape
    tm = MLP_ROW_TILE
    once = lambda arr: pl.BlockSpec(arr.shape, lambda i: (0, 0), pipeline_mode=pl.Buffered(1))
    in_specs = [pl.BlockSpec((tm, p.shape[1]), lambda i: (i, 0)) for p in parts]
    in_specs += [once(w_out), pl.BlockSpec((tm, D), lambda i: (i, 0)), once(g),
                 once(w_up), once(w_down)]
    return pl.pallas_call(
        functools.partial(_mixmlp_kernel, nparts=len(parts)),
        grid=(T // tm,),
        in_specs=in_specs,
        out_specs=pl.BlockSpec((tm, D), lambda i: (i, 0)),
        out_shape=jax.ShapeDtypeStruct((T, D), F32),
        scratch_shapes=[pltpu.VMEM((tm, D), BF16)],
        compiler_params=_params("parallel"),
        name="mix_mlp",
    )(*parts, w_out, h, g, w_up, w_down)


def _rwproj_kernel(x_ref, xh_ref, g_ref, mu_ref, wr_ref, wk_ref, wv_ref,
                   w1_ref, w2_ref, a1_ref, a2_ref, g1_ref, g2_ref, vec_ref,
                   r_ref, lw_ref, k_ref, v_ref, kk_ref, a_ref, gate_ref, *, tm, seq):
    i = pl.program_id(0)
    g = g_ref[...]
    w0, a0, k_k, k_a = (vec_ref[n:n + 1, :] for n in range(4))
    not_first = jnp.where((i * tm) % seq == 0, 0.0, 1.0)
    up = _rms(xh_ref[HALO - 1:HALO, :], g) * not_first
    u = _rms(x_ref[...], g)
    row = lax.broadcasted_iota(jnp.int32, (tm, 1), 0)
    x_prev = jnp.where(row == 0, up, pltpu.roll(u, 1, 0))
    ub = u.astype(BF16)
    xxb = (x_prev - u).astype(BF16)
    mub = mu_ref[...].astype(BF16)
    mix = lambda n: ub + xxb * mub[n:n + 1, :]

    r = jnp.dot(mix(0), wr_ref[...], preferred_element_type=F32)
    k = jnp.dot(mix(2), wk_ref[...], preferred_element_type=F32)
    v = jnp.dot(mix(3), wv_ref[...], preferred_element_type=F32)
    dw = _dot(jnp.tanh(jnp.dot(mix(1), w1_ref[...], preferred_element_type=F32)), w2_ref[...])
    a = _sigmoid(a0 + _dot(jnp.dot(mix(4), a1_ref[...], preferred_element_type=F32), a2_ref[...]))
    gate = _dot(_sigmoid(jnp.dot(mix(5), g1_ref[...], preferred_element_type=F32)), g2_ref[...])

    r_ref[...] = r.astype(r_ref.dtype)
    lw_ref[...] = (-math.exp(-0.5)) * _sigmoid(w0 + dw)
    k_ref[...] = (k * (1.0 + (a - 1.0) * k_a)).astype(k_ref.dtype)
    v_ref[...] = v.astype(v_ref.dtype)
    kk_ref[...] = (k * k_k).astype(kk_ref.dtype)
    a_ref[...] = a.astype(a_ref.dtype)
    gate_ref[...] = gate.astype(gate_ref.dtype)


def _rwkv_proj(h, g, mu, wr, wk, wv, w1, w2, a1, a2, g1, g2, vecs, seq):
    T, D = h.shape
    tm = RW_ROW_TILE
    full = lambda arr: pl.BlockSpec(arr.shape, lambda i: (0, 0), pipeline_mode=pl.Buffered(1))
    row = pl.BlockSpec((tm, D), lambda i: (i, 0))
    halo = pl.BlockSpec((HALO, D), lambda i: (jnp.maximum(i * (tm // HALO) - 1, 0), 0))
    dtypes = (BF16, F32, BF16, BF16, BF16, BF16, BF16)
    return pl.pallas_call(
        functools.partial(_rwproj_kernel, tm=tm, seq=seq),
        grid=(T // tm,),
        in_specs=[row, halo, full(g), full(mu), full(wr), full(wk), full(wv),
                  full(w1), full(w2), full(a1), full(a2), full(g1), full(g2), full(vecs)],
        out_specs=[row] * 7,
        out_shape=[jax.ShapeDtypeStruct((T, D), dt) for dt in dtypes],
        compiler_params=_params("parallel"),
        name="rwkv_proj",
    )(h, h, g, mu, wr, wk, wv, w1, w2, a1, a2, g1, g2, vecs)


def _bmm(a, b):
    return lax.dot_general(a.astype(BF16), b.astype(BF16), (((2,), (1,)), ((0,), (0,))),
                           preferred_element_type=F32)


def _bmm_nt(a, b):
    return lax.dot_general(a.astype(BF16), b.astype(BF16), (((2,), (2,)), ((0,), (0,))),
                           preferred_element_type=F32)


def _bmm_tn(a, b):
    return lax.dot_general(a.astype(BF16), b.astype(BF16), (((1,), (1,)), ((0,), (0,))),
                           preferred_element_type=F32)


def _rwscan_kernel(r_ref, lw_ref, k_ref, v_ref, kk_ref, a_ref, gate_ref,
                   rk_ref, lng_ref, lnb_ref, o_ref, s_ref, *, L, nsub, npair):
    @pl.when(pl.program_id(2) == 0)
    def _():
        s_ref[...] = jnp.zeros_like(s_ref)

    P = 2 * L
    lane = lax.broadcasted_iota(jnp.int32, (1, 1, LANES), 2)
    ri = lax.broadcasted_iota(jnp.int32, (P, P), 0)
    ci = lax.broadcasted_iota(jnp.int32, (P, P), 1)
    same = (ri >> 6) == (ci >> 6)
    gsum = jnp.where(same, 1.0, 0.0).astype(BF16)
    strict = same & (ci < ri)
    incl = same & (ci <= ri)
    ri_l = lax.broadcasted_iota(jnp.int32, (L, L), 0)
    ci_l = lax.broadcasted_iota(jnp.int32, (L, L), 1)
    tril_l = jnp.where(ci_l <= ri_l, 1.0, 0.0).astype(BF16)

    nb = npair * nsub
    rows = nsub * L

    def pairs(x):
        return jnp.concatenate([x[:, p * LANES:(p + 1) * LANES] for p in range(npair)], axis=0)

    blk = lambda ref: pairs(ref[0].astype(F32)).reshape(nb, L, LANES)
    r, lw, k, v, kk, a = (blk(x) for x in (r_ref, lw_ref, k_ref, v_ref, kk_ref, a_ref))

    def lane_group_sum(x):
        return jnp.dot(x.astype(BF16), gsum, preferred_element_type=F32)

    ssq = lane_group_sum((kk * kk).reshape(nb * L, LANES)).reshape(nb, L, LANES)
    kap = kk * lax.rsqrt(jnp.maximum(ssq, 1e-24))
    bv = kap * a

    t1, t2 = _split(lw)
    tril_b = jnp.broadcast_to(tril_l, (nb, L, L))
    cum = _bmm(tril_b, t1) + _bmm(tril_b, t2)
    c_last = cum[:, L - 1:L, :]
    w_incl = jnp.exp(cum)
    w_excl = jnp.exp(cum - lw)
    w_inv = jnp.exp(-cum)
    w_rem = jnp.exp(c_last - cum)
    w_last = jnp.exp(c_last)

    bf = lambda x: x.astype(BF16)
    head0 = lane < RW_HEAD_DIM
    zb = jnp.zeros((), BF16)
    stack = lambda x: jnp.concatenate([jnp.where(head0, x, zb), jnp.where(head0, zb, x)], axis=1)
    dup = lambda x: jnp.concatenate([x, x], axis=1)
    unstack = lambda x: x[:, :L, :] + x[:, L:, :]

    r_dec = r * w_incl
    a_st = stack(bf(-kap * w_excl))
    r_st = stack(bf(r_dec))
    v_st = stack(bf(v))
    gram = _bmm_nt(jnp.concatenate([a_st, r_st], axis=1),
                   jnp.concatenate([dup(bf(bv * w_inv)), dup(bf(k * w_inv))], axis=1))
    a_ab = bf(jnp.where(strict, gram[:, :P, :P], 0.0))
    a_ak = bf(jnp.where(strict, gram[:, :P, P:], 0.0))
    a_rb = bf(jnp.where(incl, gram[:, P:, :P], 0.0))
    a_rk = bf(jnp.where(incl, gram[:, P:, P:], 0.0))

    tinv = jnp.where(ri == ci, jnp.ones((), BF16), jnp.where((ri >> 1) == (ci >> 1), a_ab, zb))
    n = 2
    while n < L:
        sh = n.bit_length() - 1
        off = ((ri >> (sh + 1)) == (ci >> (sh + 1))) & ((ri >> sh) != (ci >> sh))
        e = jnp.where(off, a_ab, zb)
        tinv = tinv + bf(_bmm(tinv, bf(_bmm(e, tinv))))
        n *= 2

    x1 = bf(_bmm(a_ak, v_st))
    tu = bf(_bmm(tinv, jnp.concatenate([x1, a_st], axis=2)))
    yr = _bmm(a_rb, tu)
    y0 = unstack(yr[:, :, :LANES] + _bmm(a_rk, v_st))
    rq = bf(r_dec + unstack(yr[:, :, LANES:]))
    bh_st = stack(bf(bv * w_rem))
    kh_st = stack(bf(k * w_rem))
    mn = _bmm_tn(bh_st, tu)
    m_bd = bf(mn[:, :, LANES:] + jnp.where(ri == ci, w_last, 0.0))
    n_bd = mn[:, :, :LANES] + _bmm_tn(kh_st, v_st)

    s = [s_ref[p] for p in range(npair)]
    ys = [[None] * nsub for _ in range(npair)]
    for c in range(nsub):
        for p in range(npair):
            i = p * nsub + c
            ys[p][c] = _dot(rq[i], s[p]) + y0[i]
            s[p] = _dot_exact_x(m_bd[i], s[p]) + n_bd[i]
    for p in range(npair):
        s_ref[p] = s[p]
    y = jnp.concatenate([jnp.concatenate(yp, axis=0) for yp in ys], axis=0)

    lanes = lambda x: jnp.concatenate([x[p * rows:(p + 1) * rows] for p in range(npair)], axis=1)
    inv_d = 1.0 / RW_HEAD_DIM
    mean = lane_group_sum(y) * inv_d
    d = y - mean
    var = lane_group_sum(d * d) * inv_d
    yn = lanes(d * lax.rsqrt(var + GN_EPS)) * lng_ref[...] + lnb_ref[...]
    r2, k2, v2 = (x[0].astype(F32) for x in (r_ref, k_ref, v_ref))
    bonus = lanes(lane_group_sum(pairs(r2 * k2 * rk_ref[...]))) * v2
    o_ref[0] = ((yn + bonus) * gate_ref[0].astype(F32)).astype(o_ref.dtype)


def _rwkv_scan(r, lw, k, v, kk, a, gate, r_k, ln_g, ln_b):
    B, S, D = r.shape
    L, lb, npair = RW_CHUNK, RW_BLOCK, RW_PAIRS
    width = npair * LANES
    blk = pl.BlockSpec((1, lb, width), lambda b, p, j: (b, j, p))
    vec = pl.BlockSpec((1, width), lambda b, p, j: (0, p))
    return pl.pallas_call(
        functools.partial(_rwscan_kernel, L=L, nsub=lb // L, npair=npair),
        grid=(B, D // width, S // lb),
        in_specs=[blk] * 7 + [vec] * 3,
        out_specs=blk,
        out_shape=jax.ShapeDtypeStruct((B, S, D), BF16),
        scratch_shapes=[pltpu.VMEM((npair, LANES, LANES), F32)],
        compiler_params=_params("parallel", "parallel", "arbitrary"),
        name="rwkv_scan",
    )(r, lw, k, v, kk, a, gate, r_k, ln_g, ln_b)


def _even_layer(h, B, S, g, w_in, b_if, conv_w, head_g, w_out, w_up, w_down):
    T, D = h.shape
    ncols = 3 * SB_WIDTH + 4 * ML_WIDTH
    w_main = w_in[:, :ncols].astype(BF16)
    w_if = jnp.pad(w_in[:, ncols:], ((0, 0), (0, LANES - 2 * ML_HEADS))).astype(BF16)
    sb_q, sb_k, sb_v, ml_qk, ml_v, ml_o, ml_if = _even_inproj(h, g[0:1], w_main, w_if)
    to3 = lambda t: t.reshape(B, S, t.shape[-1])
    a_out = _sb_attention(to3(sb_q), to3(sb_k), to3(sb_v))
    gates_nat = to3(ml_if)
    gates_t = jnp.swapaxes(gates_nat[:, :, :2 * ML_HEADS], 1, 2)
    bias_nat = jnp.pad(b_if, (0, LANES - 2 * ML_HEADS)).reshape(1, LANES)
    bias_t = jnp.broadcast_to(b_if[:, None], (2 * ML_HEADS, ML_CHUNK))
    hm = _mlstm(to3(ml_qk), to3(ml_v), to3(ml_o), gates_nat, gates_t, bias_nat, bias_t,
                conv_w, head_g.reshape(ML_HEADS, 1, LANES))
    return _mix_mlp([a_out.reshape(T, SB_WIDTH), hm.reshape(T, ML_WIDTH)],
                    w_out.astype(BF16), h, g, w_up.astype(BF16), w_down.astype(BF16))


def _odd_layer(h, B, S, g, mu, w_rkv, w0, w1, w2, a0, a1, a2, g1, g2, k_k, k_a, r_k,
               ln_g, ln_b, w_out, w_up, w_down):
    T, D = h.shape
    bf = lambda t: t.astype(BF16)
    mu8 = jnp.pad(mu, ((0, HALO - mu.shape[0]), (0, 0)))
    vecs = jnp.pad(jnp.stack([w0, a0, k_k, k_a]), ((0, HALO - 4), (0, 0)))
    r, lw, k, v, kk, a, gate = _rwkv_proj(
        h, g[0:1], mu8, bf(w_rkv[0]), bf(w_rkv[1]), bf(w_rkv[2]),
        bf(w1), bf(w2), bf(a1), bf(a2), bf(g1), bf(g2), vecs, S)
    to3 = lambda t: t.reshape(B, S, D)
    y = _rwkv_scan(to3(r), to3(lw), to3(k), to3(v), to3(kk), to3(a), to3(gate),
                   r_k.reshape(1, D), ln_g.reshape(1, D), ln_b.reshape(1, D))
    return _mix_mlp([y.reshape(T, D)], bf(w_out), h, g, bf(w_up), bf(w_down))


def kernel(x, norm_g, e_w_in, e_b_if, e_conv_w, e_head_g, e_w_out, r_mu, r_w_rkv, r_w0, r_w1, r_w2, r_a0, r_a1, r_a2, r_g1, r_g2, r_k_k, r_k_a, r_r_k, r_ln_g, r_ln_b, r_w_out, mlp_w_up, mlp_w_down):
    B, S, D = x.shape
    h = x.reshape(B * S, D)
    for layer in range(norm_g.shape[0]):
        g = norm_g[layer]
        if layer % 2 == 0:
            e = layer // 2
            h = _even_layer(h, B, S, g, e_w_in[e], e_b_if[e], e_conv_w[e], e_head_g[e],
                            e_w_out[e], mlp_w_up[layer], mlp_w_down[layer])
        else:
            o = layer // 2
            h = _odd_layer(h, B, S, g, r_mu[o], r_w_rkv[o], r_w0[o], r_w1[o], r_w2[o],
                           r_a0[o], r_a1[o], r_a2[o], r_g1[o], r_g2[o], r_k_k[o], r_k_a[o],
                           r_r_k[o], r_ln_g[o], r_ln_b[o], r_w_out[o],
                           mlp_w_up[layer], mlp_w_down[layer])
    return h.reshape(B, S, D)
```

```python
import functools
import math

import jax
import jax.numpy as jnp
from jax import lax
from jax.experimental import pallas as pl
from jax.experimental.pallas import tpu as pltpu

F32 = jnp.float32
BF16 = jnp.bfloat16

LANES = 128
V7X_VMEM_LIMIT_BYTES = 56 * 1024 * 1024

SB_HEADS = 8
SB_HEAD_DIM = 64
SB_WIDTH = SB_HEADS * SB_HEAD_DIM
ML_HEADS = 4
ML_HEAD_DIM = 128
ML_WIDTH = ML_HEADS * ML_HEAD_DIM
CONV_WIDTH = 4
RW_HEAD_DIM = 64
SB_Q_SCALE = -math.log2(math.e) / math.sqrt(SB_HEAD_DIM)
SB_LOG2_MASS_FLOOR = -160.0
NORM_EPS = 1e-6
GN_EPS = 64e-5

ROW_TILE = 1024
RW_ROW_TILE = 512
MLP_ROW_TILE = 1024
MLP_ROW_SUB = 256
MLP_FF_TILE = 1024
SB_TILE = 256
SB_TILES_PER_STEP = 8
ML_CHUNK = 256
RW_CHUNK = 64
RW_BLOCK = 512
RW_PAIRS = 8
HALO = 8


def _params(*sem):
    return pltpu.CompilerParams(dimension_semantics=sem,
                                vmem_limit_bytes=V7X_VMEM_LIMIT_BYTES)


def _dot(a, b):
    return jnp.dot(a.astype(BF16), b.astype(BF16), preferred_element_type=F32)


def _dot_nt(a, b):
    return lax.dot_general(a.astype(BF16), b.astype(BF16), (((1,), (1,)), ((), ())),
                           preferred_element_type=F32)


def _dot_tn(a, b):
    return lax.dot_general(a.astype(BF16), b.astype(BF16), (((0,), (0,)), ((), ())),
                           preferred_element_type=F32)


def _split(x):
    hi = x.astype(BF16)
    lo = (x - hi.astype(F32)).astype(BF16)
    return hi, lo


def _dot_x_exact(x, m):
    hi, lo = _split(x)
    return (jnp.dot(hi, m, preferred_element_type=F32)
            + jnp.dot(lo, m, preferred_element_type=F32))


def _dot_exact_x(m, x):
    hi, lo = _split(x)
    return (jnp.dot(m, hi, preferred_element_type=F32)
            + jnp.dot(m, lo, preferred_element_type=F32))


def _rms(x, g):
    ms = jnp.mean(x * x, axis=-1, keepdims=True)
    return x * lax.rsqrt(ms + NORM_EPS) * g


def _softplus(z):
    return jnp.maximum(z, 0.0) + jnp.log(1.0 + jnp.exp(-jnp.abs(z)))


def _neg_abs(x):
    bits = lax.bitcast_convert_type(x, jnp.uint32) | jnp.uint32(0x80000000)
    return lax.bitcast_convert_type(bits, F32)


def _sigmoid(z):
    return 1.0 / (1.0 + jnp.exp(-z))


def _inproj_kernel(x_ref, g_ref, w_ref, wif_ref, *out_refs, widths):
    u = _rms(x_ref[...], g_ref[...]).astype(BF16)
    c0 = 0
    for n, (o_ref, w) in enumerate(zip(out_refs[:-1], widths)):
        t = jnp.dot(u, w_ref[:, c0:c0 + w], preferred_element_type=F32)
        if n == 0:
            t = t * SB_Q_SCALE
        o_ref[...] = t.astype(o_ref.dtype)
        c0 += w
    out_refs[-1][...] = jnp.dot(u, wif_ref[...], preferred_element_type=F32)


def _even_inproj(h, g, w_main, w_if):
    T, D = h.shape
    widths = (SB_WIDTH, SB_WIDTH, SB_WIDTH, 2 * ML_WIDTH, ML_WIDTH, ML_WIDTH)
    tm = ROW_TILE
    out_shape = [jax.ShapeDtypeStruct((T, w), BF16) for w in widths]
    out_shape.append(jax.ShapeDtypeStruct((T, LANES), F32))
    out_specs = [pl.BlockSpec((tm, w), lambda i: (i, 0)) for w in widths]
    out_specs.append(pl.BlockSpec((tm, LANES), lambda i: (i, 0)))
    return pl.pallas_call(
        functools.partial(_inproj_kernel, widths=widths),
        grid=(T // tm,),
        in_specs=[pl.BlockSpec((tm, D), lambda i: (i, 0)),
                  pl.BlockSpec((1, D), lambda i: (0, 0)),
                  pl.BlockSpec(w_main.shape, lambda i: (0, 0), pipeline_mode=pl.Buffered(1)),
                  pl.BlockSpec(w_if.shape, lambda i: (0, 0), pipeline_mode=pl.Buffered(1))],
        out_specs=out_specs,
        out_shape=out_shape,
        compiler_params=_params("parallel"),
        name="even_inproj",
    )(h, g, w_main, w_if)


def _sb_kernel(q_ref, k_ref, v_ref, o_ref, *, tq, ntile):
    lane = lax.broadcasted_iota(jnp.int32, (1, LANES), 1)
    first_head = lane < SB_HEAD_DIM
    rr = lax.broadcasted_iota(jnp.int32, (tq, tq), 0)
    cc = lax.broadcasted_iota(jnp.int32, (tq, tq), 1)
    lower = rr > cc
    cum_mat = jnp.where(lower, 1.0, 0.0).astype(BF16)
    heads = (first_head, jnp.logical_not(first_head))

    def weights(qh, ks, carry, diagonal):
        zs = lax.dot_general(qh, ks, (((1,), (1,)), ((), ())), preferred_element_type=F32)
        lk = jnp.minimum(zs, 0.0) - jnp.log2(1.0 + jnp.exp2(_neg_abs(zs)))
        if diagonal:
            lk = jnp.where(lower, lk, 0.0)
        lkb = lk.astype(BF16)
        cs = jnp.dot(lkb, cum_mat, preferred_element_type=F32)
        p = jnp.exp2(lk - zs + cs + carry)
        if diagonal:
            p = jnp.where(lower, p, 0.0)
        return p.astype(BF16), carry + (cs[:, 0:1] + lkb[:, 0:1].astype(F32))

    def span(qhs, start, state, diagonal):
        c0, c1, acc = state
        ks = k_ref[0, pl.ds(start, tq), :]
        vs = v_ref[0, pl.ds(start, tq), :]
        p0, c0 = weights(qhs[0], ks, c0, diagonal)
        p1, c1 = weights(qhs[1], ks, c1, diagonal)
        v01 = jnp.concatenate([jnp.where(keep, vs, jnp.zeros_like(vs)) for keep in heads], axis=0)
        acc = acc + jnp.dot(jnp.concatenate([p0, p1], axis=1), v01, preferred_element_type=F32)
        return c0, c1, acc

    col0 = jnp.zeros((tq, 1), F32)
    zero = (col0, col0, jnp.zeros((tq, LANES), F32))

    def mass_left(st):
        return (jnp.max(jnp.maximum(st[0], st[1])) > SB_LOG2_MASS_FLOOR).astype(jnp.int32)

    for t in range(ntile):
        qi = pl.program_id(2) * ntile + t
        rows = slice(t * tq, (t + 1) * tq)
        q2 = q_ref[0, rows, :]
        qhs = [jnp.where(keep, q2, jnp.zeros_like(q2)) for keep in heads]
        start0 = pl.multiple_of(qi * tq, tq)

        def with_previous_span(_):
            st = span(qhs, start0, zero, True)
            return span(qhs, pl.multiple_of(start0 - tq, tq), st, False)

        def diagonal_only(_):
            return span(qhs, start0, zero, True)

        if t == 0:
            state = lax.cond(qi > 0, with_previous_span, diagonal_only, None)
        else:
            state = with_previous_span(None)

        def cond(c):
            return (c[0] < qi) & (c[1] > 0)

        def body(c):
            j, _, st = c
            st = span(qhs, pl.multiple_of((qi - 1 - j) * tq, tq), st, False)
            return j + 1, mass_left(st), st

        _, _, state = lax.while_loop(cond, body, (jnp.int32(1), mass_left(state), state))
        o_ref[0, rows, :] = state[2].astype(o_ref.dtype)


def _sb_attention(q, k, v):
    B, S, W = q.shape
    tq, ntile = SB_TILE, SB_TILES_PER_STEP
    npair = W // LANES
    rows = tq * ntile
    return pl.pallas_call(
        functools.partial(_sb_kernel, tq=tq, ntile=ntile),
        grid=(B, npair, S // rows),
        in_specs=[pl.BlockSpec((1, rows, LANES), lambda b, p, i: (b, i, p)),
                  pl.BlockSpec((1, S, LANES), lambda b, p, i: (b, 0, p)),
                  pl.BlockSpec((1, S, LANES), lambda b, p, i: (b, 0, p))],
        out_specs=pl.BlockSpec((1, rows, LANES), lambda b, p, i: (b, i, p)),
        out_shape=jax.ShapeDtypeStruct((B, S, W), BF16),
        compiler_params=_params("parallel", "parallel", "arbitrary"),
        name="sb_attention",
    )(q, k, v)


def _mlstm_kernel(q_ref, k_ref, v_ref, o_ref, gn_ref, gt_ref, bn_ref, bt_ref,
                  cwq_ref, cwk_ref, hg_ref, out_ref,
                  c_ref, m_ref, pq_ref, pk_ref, *, L):
    chunk = pl.program_id(1)

    @pl.when(chunk == 0)
    def _():
        c_ref[...] = jnp.zeros_like(c_ref)
        m_ref[...] = jnp.zeros_like(m_ref)
        pq_ref[...] = jnp.zeros_like(pq_ref)
        pk_ref[...] = jnp.zeros_like(pk_ref)

    H = ML_HEADS
    heads = lambda x: jnp.stack([x[:, h * LANES:(h + 1) * LANES] for h in range(H)])
    unheads = lambda x: jnp.concatenate([x[h] for h in range(H)], axis=1)

    def conv_silu(x_ref, prev_ref, w):
        x = x_ref[0].astype(F32)
        xf = jnp.concatenate([prev_ref[...], x], axis=0)
        prev_ref[...] = x[L - HALO:, :]
        y = xf * w[CONV_WIDTH - 1:CONV_WIDTH, :]
        for j in range(1, CONV_WIDTH):
            y = y + pltpu.roll(xf, j, 0) * w[CONV_WIDTH - 1 - j:CONV_WIDTH - j, :]
        y = y[HALO:, :]
        return heads(y * _sigmoid(y))

    q = conv_silu(q_ref, pq_ref, cwq_ref[...])
    k = conv_silu(k_ref, pk_ref, cwk_ref[...]) * (ML_HEAD_DIM ** -0.5)
    v_aug = jnp.concatenate([heads(v_ref[0]), jnp.ones((H, L, LANES), BF16)], axis=2)

    ri = lax.broadcasted_iota(jnp.int32, (L, L), 0)
    ci = lax.broadcasted_iota(jnp.int32, (L, L), 1)
    tril = ci <= ri
    tril_m = jnp.where(tril, 1.0, 0.0).astype(BF16)
    triu_m = jnp.where(ci >= ri, 1.0, 0.0).astype(BF16)
    gn = gn_ref[0] + bn_ref[...]
    gt = gt_ref[0] + bt_ref[...]
    col = lambda x, n: jnp.stack([x[:, n + h:n + h + 1] for h in range(H)])
    row = lambda x, n: jnp.stack([x[n + h:n + h + 1, :] for h in range(H)])
    li_col = col(gn, 0)
    lf_col = -_softplus(-col(gn, H))
    li_row = row(gt, 0)
    lf_all = -_softplus(-gt)
    lf_wide = jnp.concatenate([jnp.broadcast_to(lf_col[h], (L, LANES)) for h in range(H)], axis=1)
    b_col = heads(_dot_exact_x(tril_m, lf_wide))[:, :, :1]
    b_row = row(_dot_x_exact(lf_all, triu_m), H)

    m_prev = m_ref[:, 0:1, 0:1]
    dmat = jnp.where(tril, b_col - b_row + li_row, -jnp.inf)
    inter = b_col + m_prev
    m_t = jnp.maximum(jnp.max(dmat, axis=2, keepdims=True), inter)
    scores = _bmm_nt(q, k) * jnp.exp(dmat - m_t)
    w_inter = jnp.exp(inter - m_t)
    c_aug = c_ref[...]
    num_aug = _bmm(scores, v_aug) + w_inter * _bmm(q, c_aug)
    num = num_aug[:, :, :LANES]
    den = num_aug[:, :, LANES:LANES + 1]
    hval = num / jnp.maximum(jnp.abs(den), jnp.exp(-m_t))

    b_last = b_col[:, L - 1:L, :]
    gcol = b_last - b_col + li_col
    m_new = jnp.maximum(b_last + m_prev, jnp.max(gcol, axis=1, keepdims=True))
    w_state = jnp.exp(b_last + m_prev - m_new)
    w_tok = jnp.exp(gcol - m_new)
    c_ref[...] = w_state * c_aug + _bmm_tn(k * w_tok, v_aug)
    m_ref[...] = jnp.broadcast_to(m_new, m_ref.shape)

    hn = hval * lax.rsqrt(jnp.mean(hval * hval, axis=-1, keepdims=True) + NORM_EPS) * hg_ref[...]
    out_ref[0] = (unheads(hn) * _sigmoid(o_ref[0].astype(F32))).astype(out_ref.dtype)


def _mlstm(qk, v, o, gates_nat, gates_t, bias_nat, bias_t, conv_w, head_g):
    B, S, W = v.shape
    L = ML_CHUNK
    blk = lambda off: pl.BlockSpec((1, L, W), lambda b, c: (b, c, off))
    full = lambda arr: pl.BlockSpec(arr.shape, lambda b, c: (0,) * arr.ndim)
    return pl.pallas_call(
        functools.partial(_mlstm_kernel, L=L),
        grid=(B, S // L),
        in_specs=[blk(0), blk(1), blk(0), blk(0),
                  pl.BlockSpec((1, L, LANES), lambda b, c: (b, c, 0)),
                  pl.BlockSpec((1, 2 * ML_HEADS, L), lambda b, c: (b, 0, c)),
                  full(bias_nat), full(bias_t),
                  pl.BlockSpec((CONV_WIDTH, W), lambda b, c: (0, 0)),
                  pl.BlockSpec((CONV_WIDTH, W), lambda b, c: (0, 1)),
                  full(head_g)],
        out_specs=pl.BlockSpec((1, L, W), lambda b, c: (b, c, 0)),
        out_shape=jax.ShapeDtypeStruct((B, S, W), BF16),
        scratch_shapes=[pltpu.VMEM((ML_HEADS, ML_HEAD_DIM, 2 * LANES), F32),
                        pltpu.VMEM((ML_HEADS, HALO, LANES), F32),
                        pltpu.VMEM((HALO, W), F32),
                        pltpu.VMEM((HALO, W), F32)],
        compiler_params=_params("parallel", "arbitrary"),
        name="mlstm",
    )(qk, qk, v, o, gates_nat, gates_t, bias_nat, bias_t, conv_w, conv_w, head_g)


def _mixmlp_kernel(*refs, nparts):
    parts = refs[:nparts]
    wo_ref, h_ref, g_ref, wu_ref, wd_ref, o_ref, u_ref = refs[nparts:]
    sub = MLP_ROW_SUB
    for s in range(h_ref.shape[0] // sub):
        rows = slice(s * sub, (s + 1) * sub)
        c0 = 0
        mix = None
        for p in parts:
            w = p.shape[1]
            t = jnp.dot(p[rows, :], wo_ref[c0:c0 + w, :], preferred_element_type=F32)
            mix = t if mix is None else mix + t
            c0 += w
        h1 = h_ref[rows, :] + _rms(mix, g_ref[1:2, :])
        o_ref[rows, :] = h1
        u_ref[rows, :] = _rms(h1, g_ref[2:3, :]).astype(BF16)

    u = u_ref[...]
    tf = MLP_FF_TILE
    acc = None
    for c in range(wu_ref.shape[1] // tf):
        a = jnp.maximum(jnp.dot(u, wu_ref[:, c * tf:(c + 1) * tf], preferred_element_type=F32), 0.0)
        t = jnp.dot((a * a).astype(BF16), wd_ref[c * tf:(c + 1) * tf, :], preferred_element_type=F32)
        acc = t if acc is None else acc + t
    o_ref[...] = o_ref[...] + _rms(acc, g_ref[3:4, :])


def _mix_mlp(parts, w_out, h, g, w_up, w_down):
    T, D = h.shape
    tm = MLP_ROW_TILE
    once = lambda arr: pl.BlockSpec(arr.shape, lambda i: (0, 0), pipeline_mode=pl.Buffered(1))
    in_specs = [pl.BlockSpec((tm, p.shape[1]), lambda i: (i, 0)) for p in parts]
    in_specs += [once(w_out), pl.BlockSpec((tm, D), lambda i: (i, 0)), once(g),
                 once(w_up), once(w_down)]
    return pl.pallas_call(
        functools.partial(_mixmlp_kernel, nparts=len(parts)),
        grid=(T // tm,),
        in_specs=in_specs,
        out_specs=pl.BlockSpec((tm, D), lambda i: (i, 0)),
        out_shape=jax.ShapeDtypeStruct((T, D), F32),
        scratch_shapes=[pltpu.VMEM((tm, D), BF16)],
        compiler_params=_params("parallel"),
        name="mix_mlp",
    )(*parts, w_out, h, g, w_up, w_down)


def _rwproj_kernel(x_ref, xh_ref, g_ref, mu_ref, wr_ref, wk_ref, wv_ref,
                   w1_ref, w2_ref, a1_ref, a2_ref, g1_ref, g2_ref, vec_ref,
                   r_ref, lw_ref, k_ref, v_ref, kk_ref, a_ref, gate_ref, *, tm, seq):
    i = pl.program_id(0)
    g = g_ref[...]
    w0, a0, k_k, k_a = (vec_ref[n:n + 1, :] for n in range(4))
    not_first = jnp.where((i * tm) % seq == 0, 0.0, 1.0)
    up = _rms(xh_ref[HALO - 1:HALO, :], g) * not_first
    u = _rms(x_ref[...], g)
    row = lax.broadcasted_iota(jnp.int32, (tm, 1), 0)
    x_prev = jnp.where(row == 0, up, pltpu.roll(u, 1, 0))
    ub = u.astype(BF16)
    xxb = (x_prev - u).astype(BF16)
    mub = mu_ref[...].astype(BF16)
    mix = lambda n: ub + xxb * mub[n:n + 1, :]

    r = jnp.dot(mix(0), wr_ref[...], preferred_element_type=F32)
    k = jnp.dot(mix(2), wk_ref[...], preferred_element_type=F32)
    v = jnp.dot(mix(3), wv_ref[...], preferred_element_type=F32)
    dw = _dot(jnp.tanh(jnp.dot(mix(1), w1_ref[...], preferred_element_type=F32)), w2_ref[...])
    a = _sigmoid(a0 + _dot(jnp.dot(mix(4), a1_ref[...], preferred_element_type=F32), a2_ref[...]))
    gate = _dot(_sigmoid(jnp.dot(mix(5), g1_ref[...], preferred_element_type=F32)), g2_ref[...])

    r_ref[...] = r.astype(r_ref.dtype)
    lw_ref[...] = (-math.exp(-0.5)) * _sigmoid(w0 + dw)
    k_ref[...] = (k * (1.0 + (a - 1.0) * k_a)).astype(k_ref.dtype)
    v_ref[...] = v.astype(v_ref.dtype)
    kk_ref[...] = (k * k_k).astype(kk_ref.dtype)
    a_ref[...] = a.astype(a_ref.dtype)
    gate_ref[...] = gate.astype(gate_ref.dtype)


def _rwkv_proj(h, g, mu, wr, wk, wv, w1, w2, a1, a2, g1, g2, vecs, seq):
    T, D = h.shape
    tm = RW_ROW_TILE
    full = lambda arr: pl.BlockSpec(arr.shape, lambda i: (0, 0), pipeline_mode=pl.Buffered(1))
    row = pl.BlockSpec((tm, D), lambda i: (i, 0))
    halo = pl.BlockSpec((HALO, D), lambda i: (jnp.maximum(i * (tm // HALO) - 1, 0), 0))
    dtypes = (BF16, F32, BF16, BF16, BF16, BF16, BF16)
    return pl.pallas_call(
        functools.partial(_rwproj_kernel, tm=tm, seq=seq),
        grid=(T // tm,),
        in_specs=[row, halo, full(g), full(mu), full(wr), full(wk), full(wv),
                  full(w1), full(w2), full(a1), full(a2), full(g1), full(g2), full(vecs)],
        out_specs=[row] * 7,
        out_shape=[jax.ShapeDtypeStruct((T, D), dt) for dt in dtypes],
        compiler_params=_params("parallel"),
        name="rwkv_proj",
    )(h, h, g, mu, wr, wk, wv, w1, w2, a1, a2, g1, g2, vecs)


def _bmm(a, b):
    return lax.dot_general(a.astype(BF16), b.astype(BF16), (((2,), (1,)), ((0,), (0,))),
                           preferred_element_type=F32)


def _bmm_nt(a, b):
    return lax.dot_general(a.astype(BF16), b.astype(BF16), (((2,), (2,)), ((0,), (0,))),
                           preferred_element_type=F32)


def _bmm_tn(a, b):
    return lax.dot_general(a.astype(BF16), b.astype(BF16), (((1,), (1,)), ((0,), (0,))),
                           preferred_element_type=F32)


def _rwscan_kernel(r_ref, lw_ref, k_ref, v_ref, kk_ref, a_ref, gate_ref,
                   rk_ref, lng_ref, lnb_ref, o_ref, s_ref, *, L, nsub, npair):
    @pl.when(pl.program_id(2) == 0)
    def _():
        s_ref[...] = jnp.zeros_like(s_ref)

    P = 2 * L
    lane = lax.broadcasted_iota(jnp.int32, (1, 1, LANES), 2)
    ri = lax.broadcasted_iota(jnp.int32, (P, P), 0)
    ci = lax.broadcasted_iota(jnp.int32, (P, P), 1)
    same = (ri >> 6) == (ci >> 6)
    gsum = jnp.where(same, 1.0, 0.0).astype(BF16)
    strict = same & (ci < ri)
    incl = same & (ci <= ri)
    ri_l = lax.broadcasted_iota(jnp.int32, (L, L), 0)
    ci_l = lax.broadcasted_iota(jnp.int32, (L, L), 1)
    tril_l = jnp.where(ci_l <= ri_l, 1.0, 0.0).astype(BF16)

    nb = npair * nsub
    rows = nsub * L

    def pairs(x):
        return jnp.concatenate([x[:, p * LANES:(p + 1) * LANES] for p in range(npair)], axis=0)

    blk = lambda ref: pairs(ref[0].astype(F32)).reshape(nb, L, LANES)
    r, lw, k, v, kk, a = (blk(x) for x in (r_ref, lw_ref, k_ref, v_ref, kk_ref, a_ref))

    def lane_group_sum(x):
        return jnp.dot(x.astype(BF16), gsum, preferred_element_type=F32)

    ssq = lane_group_sum((kk * kk).reshape(nb * L, LANES)).reshape(nb, L, LANES)
    kap = kk * lax.rsqrt(jnp.maximum(ssq, 1e-24))
    bv = kap * a

    t1, t2 = _split(lw)
    tril_b = jnp.broadcast_to(tril_l, (nb, L, L))
    cum = _bmm(tril_b, t1) + _bmm(tril_b, t2)
    c_last = cum[:, L - 1:L, :]
    w_incl = jnp.exp(cum)
    w_excl = jnp.exp(cum - lw)
    w_inv = jnp.exp(-cum)
    w_rem = jnp.exp(c_last - cum)
    w_last = jnp.exp(c_last)

    bf = lambda x: x.astype(BF16)
    head0 = lane < RW_HEAD_DIM
    zb = jnp.zeros((), BF16)
    stack = lambda x: jnp.concatenate([jnp.where(head0, x, zb), jnp.where(head0, zb, x)], axis=1)
    dup = lambda x: jnp.concatenate([x, x], axis=1)
    unstack = lambda x: x[:, :L, :] + x[:, L:, :]

    r_dec = r * w_incl
    a_st = stack(bf(-kap * w_excl))
    r_st = stack(bf(r_dec))
    v_st = stack(bf(v))
    gram = _bmm_nt(jnp.concatenate([a_st, r_st], axis=1),
                   jnp.concatenate([dup(bf(bv * w_inv)), dup(bf(k * w_inv))], axis=1))
    a_ab = bf(jnp.where(strict, gram[:, :P, :P], 0.0))
    a_ak = bf(jnp.where(strict, gram[:, :P, P:], 0.0))
    a_rb = bf(jnp.where(incl, gram[:, P:, :P], 0.0))
    a_rk = bf(jnp.where(incl, gram[:, P:, P:], 0.0))

    tinv = jnp.where(ri == ci, jnp.ones((), BF16), jnp.where((ri >> 1) == (ci >> 1), a_ab, zb))
    n = 2
    while n < L:
        sh = n.bit_length() - 1
        off = ((ri >> (sh + 1)) == (ci >> (sh + 1))) & ((ri >> sh) != (ci >> sh))
        e = jnp.where(off, a_ab, zb)
        tinv = tinv + bf(_bmm(tinv, bf(_bmm(e, tinv))))
        n *= 2

    x1 = bf(_bmm(a_ak, v_st))
    tu = bf(_bmm(tinv, jnp.concatenate([x1, a_st], axis=2)))
    yr = _bmm(a_rb, tu)
    y0 = unstack(yr[:, :, :LANES] + _bmm(a_rk, v_st))
    rq = bf(r_dec + unstack(yr[:, :, LANES:]))
    bh_st = stack(bf(bv * w_rem))
    kh_st = stack(bf(k * w_rem))
    mn = _bmm_tn(bh_st, tu)
    m_bd = bf(mn[:, :, LANES:] + jnp.where(ri == ci, w_last, 0.0))
    n_bd = mn[:, :, :LANES] + _bmm_tn(kh_st, v_st)

    s = [s_ref[p] for p in range(npair)]
    ys = [[None] * nsub for _ in range(npair)]
    for c in range(nsub):
        for p in range(npair):
            i = p * nsub + c
            ys[p][c] = _dot(rq[i], s[p]) + y0[i]
            s[p] = _dot_exact_x(m_bd[i], s[p]) + n_bd[i]
    for p in range(npair):
        s_ref[p] = s[p]
    y = jnp.concatenate([jnp.concatenate(yp, axis=0) for yp in ys], axis=0)

    lanes = lambda x: jnp.concatenate([x[p * rows:(p + 1) * rows] for p in range(npair)], axis=1)
    inv_d = 1.0 / RW_HEAD_DIM
    mean = lane_group_sum(y) * inv_d
    d = y - mean
    var = lane_group_sum(d * d) * inv_d
    yn = lanes(d * lax.rsqrt(var + GN_EPS)) * lng_ref[...] + lnb_ref[...]
    r2, k2, v2 = (x[0].astype(F32) for x in (r_ref, k_ref, v_ref))
    bonus = lanes(lane_group_sum(pairs(r2 * k2 * rk_ref[...]))) * v2
    o_ref[0] = ((yn + bonus) * gate_ref[0].astype(F32)).astype(o_ref.dtype)


def _rwkv_scan(r, lw, k, v, kk, a, gate, r_k, ln_g, ln_b):
    B, S, D = r.shape
    L, lb, npair = RW_CHUNK, RW_BLOCK, RW_PAIRS
    width = npair * LANES
    blk = pl.BlockSpec((1, lb, width), lambda b, p, j: (b, j, p))
    vec = pl.BlockSpec((1, width), lambda b, p, j: (0, p))
    return pl.pallas_call(
        functools.partial(_rwscan_kernel, L=L, nsub=lb // L, npair=npair),
        grid=(B, D // width, S // lb),
        in_specs=[blk] * 7 + [vec] * 3,
        out_specs=blk,
        out_shape=jax.ShapeDtypeStruct((B, S, D), BF16),
        scratch_shapes=[pltpu.VMEM((npair, LANES, LANES), F32)],
        compiler_params=_params("parallel", "parallel", "arbitrary"),
        name="rwkv_scan",
    )(r, lw, k, v, kk, a, gate, r_k, ln_g, ln_b)


def _even_layer(h, B, S, g, w_in, b_if, conv_w, head_g, w_out, w_up, w_down):
    T, D = h.shape
    ncols = 3 * SB_WIDTH + 4 * ML_WIDTH
    w_main = w_in[:, :ncols].astype(BF16)
    w_if = jnp.pad(w_in[:, ncols:], ((0, 0), (0, LANES - 2 * ML_HEADS))).astype(BF16)
    sb_q, sb_k, sb_v, ml_qk, ml_v, ml_o, ml_if = _even_inproj(h, g[0:1], w_main, w_if)
    to3 = lambda t: t.reshape(B, S, t.shape[-1])
    a_out = _sb_attention(to3(sb_q), to3(sb_k), to3(sb_v))
    gates_nat = to3(ml_if)
    gates_t = jnp.swapaxes(gates_nat[:, :, :2 * ML_HEADS], 1, 2)
    bias_nat = jnp.pad(b_if, (0, LANES - 2 * ML_HEADS)).reshape(1, LANES)
    bias_t = jnp.broadcast_to(b_if[:, None], (2 * ML_HEADS, ML_CHUNK))
    hm = _mlstm(to3(ml_qk), to3(ml_v), to3(ml_o), gates_nat, gates_t, bias_nat, bias_t,
                conv_w, head_g.reshape(ML_HEADS, 1, LANES))
    return _mix_mlp([a_out.reshape(T, SB_WIDTH), hm.reshape(T, ML_WIDTH)],
                    w_out.astype(BF16), h, g, w_up.astype(BF16), w_down.astype(BF16))


def _odd_layer(h, B, S, g, mu, w_rkv, w0, w1, w2, a0, a1, a2, g1, g2, k_k, k_a, r_k,
               ln_g, ln_b, w_out, w_up, w_down):
    T, D = h.shape
    bf = lambda t: t.astype(BF16)
    mu8 = jnp.pad(mu, ((0, HALO - mu.shape[0]), (0, 0)))
    vecs = jnp.pad(jnp.stack([w0, a0, k_k, k_a]), ((0, HALO - 4), (0, 0)))
    r, lw, k, v, kk, a, gate = _rwkv_proj(
        h, g[0:1], mu8, bf(w_rkv[0]), bf(w_rkv[1]), bf(w_rkv[2]),
        bf(w1), bf(w2), bf(a1), bf(a2), bf(g1), bf(g2), vecs, S)
    to3 = lambda t: t.reshape(B, S, D)
    y = _rwkv_scan(to3(r), to3(lw), to3(k), to3(v), to3(kk), to3(a), to3(gate),
                   r_k.reshape(1, D), ln_g.reshape(1, D), ln_b.reshape(1, D))
    return _mix_mlp([y.reshape(T, D)], bf(w_out), h, g, bf(w_up), bf(w_down))


def kernel(x, norm_g, e_w_in, e_b_if, e_conv_w, e_head_g, e_w_out, r_mu, r_w_rkv, r_w0, r_w1, r_w2, r_a0, r_a1, r_a2, r_g1, r_g2, r_k_k, r_k_a, r_r_k, r_ln_g, r_ln_b, r_w_out, mlp_w_up, mlp_w_down):
    B, S, D = x.shape
    h = x.reshape(B * S, D)
    for layer in range(norm_g.shape[0]):
        g = norm_g[layer]
        if layer % 2 == 0:
            e = layer // 2
            h = _even_layer(h, B, S, g, e_w_in[e], e_b_if[e], e_conv_w[e], e_head_g[e],
                            e_w_out[e], mlp_w_up[layer], mlp_w_down[layer])
        else:
            o = layer // 2
            h = _odd_layer(h, B, S, g, r_mu[o], r_w_rkv[o], r_w0[o], r_w1[o], r_w2[o],
                           r_a0[o], r_a1[o], r_a2[o], r_g1[o], r_g2[o], r_k_k[o], r_k_a[o],
                           r_r_k[o], r_ln_g[o], r_ln_b[o], r_w_out[o],
                           mlp_w_up[layer], mlp_w_down[layer])
    return h.reshape(B, S, D)
```

```python
import functools
import math

import jax
import jax.numpy as jnp
from jax import lax
from jax.experimental import pallas as pl
from jax.experimental.pallas import tpu as pltpu

F32 = jnp.float32
BF16 = jnp.bfloat16

LANES = 128
V7X_VMEM_LIMIT_BYTES = 56 * 1024 * 1024

SB_HEADS = 8
SB_HEAD_DIM = 64
SB_WIDTH = SB_HEADS * SB_HEAD_DIM
ML_HEADS = 4
ML_HEAD_DIM = 128
ML_WIDTH = ML_HEADS * ML_HEAD_DIM
CONV_WIDTH = 4
RW_HEAD_DIM = 64
SB_Q_SCALE = -math.log2(math.e) / math.sqrt(SB_HEAD_DIM)
SB_LOG2_MASS_FLOOR = -160.0
NORM_EPS = 1e-6
GN_EPS = 64e-5

ROW_TILE = 1024
RW_ROW_TILE = 512
MLP_ROW_TILE = 1024
MLP_ROW_SUB = 256
MLP_FF_TILE = 1024
SB_TILE = 256
SB_TILES_PER_STEP = 8
ML_CHUNK = 256
ML_CHUNKS_PER_STEP = 2
RW_CHUNK = 64
RW_BLOCK = 512
RW_PAIRS = 8
HALO = 8


def _params(*sem):
    return pltpu.CompilerParams(dimension_semantics=sem,
                                vmem_limit_bytes=V7X_VMEM_LIMIT_BYTES)


def _dot(a, b):
    return jnp.dot(a.astype(BF16), b.astype(BF16), preferred_element_type=F32)


def _split(x):
    hi = x.astype(BF16)
    lo = (x - hi.astype(F32)).astype(BF16)
    return hi, lo


def _dot_x_exact(x, m):
    hi, lo = _split(x)
    return (jnp.dot(hi, m, preferred_element_type=F32)
            + jnp.dot(lo, m, preferred_element_type=F32))


def _dot_exact_x(m, x):
    hi, lo = _split(x)
    return (jnp.dot(m, hi, preferred_element_type=F32)
            + jnp.dot(m, lo, preferred_element_type=F32))


def _rms(x, g):
    ms = jnp.mean(x * x, axis=-1, keepdims=True)
    return x * lax.rsqrt(ms + NORM_EPS) * g


def _softplus(z):
    return jnp.maximum(z, 0.0) + jnp.log(1.0 + jnp.exp(-jnp.abs(z)))


def _neg_abs(x):
    bits = lax.bitcast_convert_type(x, jnp.uint32) | jnp.uint32(0x80000000)
    return lax.bitcast_convert_type(bits, F32)


def _sigmoid(z):
    return 1.0 / (1.0 + jnp.exp(-z))


def _inproj_kernel(x_ref, g_ref, w_ref, wif_ref, *out_refs, widths):
    u = _rms(x_ref[...], g_ref[...]).astype(BF16)
    c0 = 0
    for n, (o_ref, w) in enumerate(zip(out_refs[:-1], widths)):
        t = jnp.dot(u, w_ref[:, c0:c0 + w], preferred_element_type=F32)
        if n == 0:
            t = t * SB_Q_SCALE
        o_ref[...] = t.astype(o_ref.dtype)
        c0 += w
    out_refs[-1][...] = jnp.dot(u, wif_ref[...], preferred_element_type=F32)


def _even_inproj(h, g, w_main, w_if):
    T, D = h.shape
    widths = (SB_WIDTH, SB_WIDTH, SB_WIDTH, 2 * ML_WIDTH, ML_WIDTH, ML_WIDTH)
    tm = ROW_TILE
    assert T % tm == 0
    out_shape = [jax.ShapeDtypeStruct((T, w), BF16) for w in widths]
    out_shape.append(jax.ShapeDtypeStruct((T, LANES), F32))
    out_specs = [pl.BlockSpec((tm, w), lambda i: (i, 0)) for w in widths]
    out_specs.append(pl.BlockSpec((tm, LANES), lambda i: (i, 0)))
    return pl.pallas_call(
        functools.partial(_inproj_kernel, widths=widths),
        grid=(T // tm,),
        in_specs=[pl.BlockSpec((tm, D), lambda i: (i, 0)),
                  pl.BlockSpec((1, D), lambda i: (0, 0)),
                  pl.BlockSpec(w_main.shape, lambda i: (0, 0), pipeline_mode=pl.Buffered(1)),
                  pl.BlockSpec(w_if.shape, lambda i: (0, 0), pipeline_mode=pl.Buffered(1))],
        out_specs=out_specs,
        out_shape=out_shape,
        compiler_params=_params("parallel"),
        name="even_inproj",
    )(h, g, w_main, w_if)


def _sb_kernel(q_ref, k_ref, v_ref, o_ref, *, tq, ntile):
    lane = lax.broadcasted_iota(jnp.int32, (1, LANES), 1)
    first_head = lane < SB_HEAD_DIM
    rr = lax.broadcasted_iota(jnp.int32, (tq, tq), 0)
    cc = lax.broadcasted_iota(jnp.int32, (tq, tq), 1)
    lower = rr > cc
    cum_mat = jnp.where(lower, 1.0, 0.0).astype(BF16)
    heads = (first_head, jnp.logical_not(first_head))

    def weights(qh, ks, carry, diagonal):
        zs = lax.dot_general(qh, ks, (((1,), (1,)), ((), ())), preferred_element_type=F32)
        lk = jnp.minimum(zs, 0.0) - jnp.log2(1.0 + jnp.exp2(_neg_abs(zs)))
        if diagonal:
            lk = jnp.where(lower, lk, 0.0)
        lkb = lk.astype(BF16)
        cs = jnp.dot(lkb, cum_mat, preferred_element_type=F32)
        p = jnp.exp2(lk - zs + cs + carry)
        if diagonal:
            p = jnp.where(lower, p, 0.0)
        return p.astype(BF16), carry + (cs[:, 0:1] + lkb[:, 0:1].astype(F32))

    def span(qhs, start, state, diagonal):
        c0, c1, acc = state
        ks = k_ref[0, pl.ds(start, tq), :]
        vs = v_ref[0, pl.ds(start, tq), :]
        p0, c0 = weights(qhs[0], ks, c0, diagonal)
        p1, c1 = weights(qhs[1], ks, c1, diagonal)
        v01 = jnp.concatenate([jnp.where(keep, vs, jnp.zeros_like(vs)) for keep in heads], axis=0)
        acc = acc + jnp.dot(jnp.concatenate([p0, p1], axis=1), v01, preferred_element_type=F32)
        return c0, c1, acc

    col0 = jnp.zeros((tq, 1), F32)
    zero = (col0, col0, jnp.zeros((tq, LANES), F32))

    def mass_left(st):
        return (jnp.max(jnp.maximum(st[0], st[1])) > SB_LOG2_MASS_FLOOR).astype(jnp.int32)

    for t in range(ntile):
        qi = pl.program_id(2) * ntile + t
        rows = slice(t * tq, (t + 1) * tq)
        q2 = q_ref[0, rows, :]
        qhs = [jnp.where(keep, q2, jnp.zeros_like(q2)) for keep in heads]
        start0 = pl.multiple_of(qi * tq, tq)

        def with_previous_span(_):
            st = span(qhs, start0, zero, True)
            return span(qhs, pl.multiple_of(start0 - tq, tq), st, False)

        def diagonal_only(_):
            return span(qhs, start0, zero, True)

        if t == 0:
            state = lax.cond(qi > 0, with_previous_span, diagonal_only, None)
        else:
            state = with_previous_span(None)

        def cond(c):
            return (c[0] < qi) & (c[1] > 0)

        def body(c):
            j, _, st = c
            st = span(qhs, pl.multiple_of((qi - 1 - j) * tq, tq), st, False)
            return j + 1, mass_left(st), st

        _, _, state = lax.while_loop(cond, body, (jnp.int32(1), mass_left(state), state))
        o_ref[0, rows, :] = state[2].astype(o_ref.dtype)


def _sb_attention(q, k, v):
    B, S, W = q.shape
    tq, ntile = SB_TILE, SB_TILES_PER_STEP
    npair = W // LANES
    rows = tq * ntile
    assert S % rows == 0 and W % LANES == 0
    return pl.pallas_call(
        functools.partial(_sb_kernel, tq=tq, ntile=ntile),
        grid=(B, npair, S // rows),
        in_specs=[pl.BlockSpec((1, rows, LANES), lambda b, p, i: (b, i, p)),
                  pl.BlockSpec((1, S, LANES), lambda b, p, i: (b, 0, p)),
                  pl.BlockSpec((1, S, LANES), lambda b, p, i: (b, 0, p))],
        out_specs=pl.BlockSpec((1, rows, LANES), lambda b, p, i: (b, i, p)),
        out_shape=jax.ShapeDtypeStruct((B, S, W), BF16),
        compiler_params=_params("parallel", "parallel", "arbitrary"),
        name="sb_attention",
    )(q, k, v)


def _mlstm_kernel(q_ref, k_ref, v_ref, o_ref, gn_ref, gt_ref, bn_ref, bt_ref,
                  cwq_ref, cwk_ref, hg_ref, out_ref,
                  c_ref, m_ref, pq_ref, pk_ref, *, L, first):
    if first is not None:
        @pl.when(first)
        def _():
            c_ref[...] = jnp.zeros_like(c_ref)
            m_ref[...] = jnp.zeros_like(m_ref)
            pq_ref[...] = jnp.zeros_like(pq_ref)
            pk_ref[...] = jnp.zeros_like(pk_ref)

    H = ML_HEADS
    heads = lambda x: jnp.stack([x[:, h * LANES:(h + 1) * LANES] for h in range(H)])
    unheads = lambda x: jnp.concatenate([x[h] for h in range(H)], axis=1)

    def conv_silu(x_ref, prev_ref, w):
        x = x_ref[0].astype(F32)
        xf = jnp.concatenate([prev_ref[...], x], axis=0)
        prev_ref[...] = x[L - HALO:, :]
        y = xf * w[CONV_WIDTH - 1:CONV_WIDTH, :]
        for j in range(1, CONV_WIDTH):
            y = y + pltpu.roll(xf, j, 0) * w[CONV_WIDTH - 1 - j:CONV_WIDTH - j, :]
        y = y[HALO:, :]
        return heads(y * _sigmoid(y))

    q = conv_silu(q_ref, pq_ref, cwq_ref[...])
    k = conv_silu(k_ref, pk_ref, cwk_ref[...]) * (ML_HEAD_DIM ** -0.5)
    v_aug = jnp.concatenate([heads(v_ref[0]), jnp.ones((H, L, LANES), BF16)], axis=2)

    ri = lax.broadcasted_iota(jnp.int32, (L, L), 0)
    ci = lax.broadcasted_iota(jnp.int32, (L, L), 1)
    tril = ci <= ri
    tril_m = jnp.where(tril, 1.0, 0.0).astype(BF16)
    triu_m = jnp.where(ci >= ri, 1.0, 0.0).astype(BF16)
    gn = gn_ref[0] + bn_ref[...]
    gt = gt_ref[0] + bt_ref[...]
    col = lambda x, n: jnp.stack([x[:, n + h:n + h + 1] for h in range(H)])
    row = lambda x, n: jnp.stack([x[n + h:n + h + 1, :] for h in range(H)])
    li_col = col(gn, 0)
    lf_col = -_softplus(-col(gn, H))
    li_row = row(gt, 0)
    lf_all = -_softplus(-gt)
    lf_wide = jnp.concatenate([jnp.broadcast_to(lf_col[h], (L, LANES)) for h in range(H)], axis=1)
    b_col = heads(_dot_exact_x(tril_m, lf_wide))[:, :, :1]
    b_row = row(_dot_x_exact(lf_all, triu_m), H)

    m_prev = m_ref[:, 0:1, 0:1]
    dmat = jnp.where(tril, b_col - b_row + li_row, -jnp.inf)
    inter = b_col + m_prev
    m_t = jnp.maximum(jnp.max(dmat, axis=2, keepdims=True), inter)
    scores = _bmm_nt(q, k) * jnp.exp(dmat - m_t)
    w_inter = jnp.exp(inter - m_t)
    c_aug = c_ref[...]
    num_aug = _bmm(scores, v_aug) + w_inter * _bmm(q, c_aug)
    num = num_aug[:, :, :LANES]
    den = num_aug[:, :, LANES:LANES + 1]
    hval = num / jnp.maximum(jnp.abs(den), jnp.exp(-m_t))

    b_last = b_col[:, L - 1:L, :]
    gcol = b_last - b_col + li_col
    m_new = jnp.maximum(b_last + m_prev, jnp.max(gcol, axis=1, keepdims=True))
    w_state = jnp.exp(b_last + m_prev - m_new)
    w_tok = jnp.exp(gcol - m_new)
    c_ref[...] = w_state * c_aug + _bmm_tn(k * w_tok, v_aug)
    m_ref[...] = jnp.broadcast_to(m_new, m_ref.shape)

    hn = hval * lax.rsqrt(jnp.mean(hval * hval, axis=-1, keepdims=True) + NORM_EPS) * hg_ref[...]
    out_ref[0] = (unheads(hn) * _sigmoid(o_ref[0].astype(F32))).astype(out_ref.dtype)


def _mlstm_step_kernel(q_ref, k_ref, v_ref, o_ref, gn_ref, gt_ref, *rest, L, nchunk):
    for s in range(nchunk):
        rows = pl.ds(s * L, L)
        sub = lambda ref: ref.at[:, rows, :]
        *consts, out_ref, c_ref, m_ref, pq_ref, pk_ref = rest
        _mlstm_kernel(sub(q_ref), sub(k_ref), sub(v_ref), sub(o_ref), sub(gn_ref),
                      gt_ref.at[:, :, rows], *consts, sub(out_ref), c_ref, m_ref, pq_ref, pk_ref,
                      L=L, first=(pl.program_id(1) == 0) if s == 0 else None)


def _mlstm(qk, v, o, gates_nat, gates_t, bias_nat, bias_t, conv_w, head_g):
    B, S, W = v.shape
    L, nchunk = ML_CHUNK, ML_CHUNKS_PER_STEP
    rows = L * nchunk
    assert S % rows == 0
    blk = lambda off: pl.BlockSpec((1, rows, W), lambda b, c: (b, c, off))
    full = lambda arr: pl.BlockSpec(arr.shape, lambda b, c: (0,) * arr.ndim)
    return pl.pallas_call(
        functools.partial(_mlstm_step_kernel, L=L, nchunk=nchunk),
        grid=(B, S // rows),
        in_specs=[blk(0), blk(1), blk(0), blk(0),
                  pl.BlockSpec((1, rows, LANES), lambda b, c: (b, c, 0)),
                  pl.BlockSpec((1, 2 * ML_HEADS, rows), lambda b, c: (b, 0, c)),
                  full(bias_nat), full(bias_t),
                  pl.BlockSpec((CONV_WIDTH, W), lambda b, c: (0, 0)),
                  pl.BlockSpec((CONV_WIDTH, W), lambda b, c: (0, 1)),
                  full(head_g)],
        out_specs=pl.BlockSpec((1, rows, W), lambda b, c: (b, c, 0)),
        out_shape=jax.ShapeDtypeStruct((B, S, W), BF16),
        scratch_shapes=[pltpu.VMEM((ML_HEADS, ML_HEAD_DIM, 2 * LANES), F32),
                        pltpu.VMEM((ML_HEADS, HALO, LANES), F32),
                        pltpu.VMEM((HALO, W), F32),
                        pltpu.VMEM((HALO, W), F32)],
        compiler_params=_params("parallel", "arbitrary"),
        name="mlstm",
    )(qk, qk, v, o, gates_nat, gates_t, bias_nat, bias_t, conv_w, conv_w, head_g)


def _mixmlp_kernel(*refs, nparts):
    parts = refs[:nparts]
    wo_ref, h_ref, g_ref, wu_ref, wd_ref, o_ref, u_ref = refs[nparts:]
    sub = MLP_ROW_SUB
    for s in range(h_ref.shape[0] // sub):
        rows = slice(s * sub, (s + 1) * sub)
        c0 = 0
        mix = None
        for p in parts:
            w = p.shape[1]
            t = jnp.dot(p[rows, :], wo_ref[c0:c0 + w, :], preferred_element_type=F32)
            mix = t if mix is None else mix + t
            c0 += w
        h1 = h_ref[rows, :] + _rms(mix, g_ref[1:2, :])
        o_ref[rows, :] = h1
        u_ref[rows, :] = _rms(h1, g_ref[2:3, :]).astype(BF16)

    u = u_ref[...]
    tf = MLP_FF_TILE
    acc = None
    for c in range(wu_ref.shape[1] // tf):
        a = jnp.maximum(jnp.dot(u, wu_ref[:, c * tf:(c + 1) * tf], preferred_element_type=F32), 0.0)
        t = jnp.dot((a * a).astype(BF16), wd_ref[c * tf:(c + 1) * tf, :], preferred_element_type=F32)
        acc = t if acc is None else acc + t
    o_ref[...] = o_ref[...] + _rms(acc, g_ref[3:4, :])


def _mix_mlp(parts, w_out, h, g, w_up, w_down):
    T, D = h.shape
    tm = MLP_ROW_TILE
    assert T % tm == 0 and tm % MLP_ROW_SUB == 0 and w_up.shape[1] % MLP_FF_TILE == 0
    once = lambda arr: pl.BlockSpec(arr.shape, lambda i: (0, 0), pipeline_mode=pl.Buffered(1))
    in_specs = [pl.BlockSpec((tm, p.shape[1]), lambda i: (i, 0)) for p in parts]
    in_specs += [once(w_out), pl.BlockSpec((tm, D), lambda i: (i, 0)), once(g),
                 once(w_up), once(w_down)]
    return pl.pallas_call(
        functools.partial(_mixmlp_kernel, nparts=len(parts)),
        grid=(T // tm,),
        in_specs=in_specs,
        out_specs=pl.BlockSpec((tm, D), lambda i: (i, 0)),
        out_shape=jax.ShapeDtypeStruct((T, D), F32),
        scratch_shapes=[pltpu.VMEM((tm, D), BF16)],
        compiler_params=_params("parallel"),
        name="mix_mlp",
    )(*parts, w_out, h, g, w_up, w_down)


def _rwproj_kernel(x_ref, xh_ref, g_ref, mu_ref, wr_ref, wk_ref, wv_ref,
                   w1_ref, w2_ref, a1_ref, a2_ref, g1_ref, g2_ref, vec_ref,
                   r_ref, lw_ref, k_ref, v_ref, kk_ref, a_ref, gate_ref, *, tm, seq):
    i = pl.program_id(0)
    g = g_ref[...]
    w0, a0, k_k, k_a = (vec_ref[n:n + 1, :] for n in range(4))
    not_first = jnp.where((i * tm) % seq == 0, 0.0, 1.0)
    up = _rms(xh_ref[HALO - 1:HALO, :], g) * not_first
    u = _rms(x_ref[...], g)
    row = lax.broadcasted_iota(jnp.int32, (tm, 1), 0)
    x_prev = jnp.where(row == 0, up, pltpu.roll(u, 1, 0))
    ub = u.astype(BF16)
    xxb = (x_prev - u).astype(BF16)
    mub = mu_ref[...].astype(BF16)
    mix = lambda n: ub + xxb * mub[n:n + 1, :]

    r = jnp.dot(mix(0), wr_ref[...], preferred_element_type=F32)
    k = jnp.dot(mix(2), wk_ref[...], preferred_element_type=F32)
    v = jnp.dot(mix(3), wv_ref[...], preferred_element_type=F32)
    dw = _dot(jnp.tanh(jnp.dot(mix(1), w1_ref[...], preferred_element_type=F32)), w2_ref[...])
    a = _sigmoid(a0 + _dot(jnp.dot(mix(4), a1_ref[...], preferred_element_type=F32), a2_ref[...]))
    gate = _dot(_sigmoid(jnp.dot(mix(5), g1_ref[...], preferred_element_type=F32)), g2_ref[...])

    r_ref[...] = r.astype(r_ref.dtype)
    lw_ref[...] = (-math.exp(-0.5)) * _sigmoid(w0 + dw)
    k_ref[...] = (k * (1.0 + (a - 1.0) * k_a)).astype(k_ref.dtype)
    v_ref[...] = v.astype(v_ref.dtype)
    kk_ref[...] = (k * k_k).astype(kk_ref.dtype)
    a_ref[...] = a.astype(a_ref.dtype)
    gate_ref[...] = gate.astype(gate_ref.dtype)


def _rwkv_proj(h, g, mu, wr, wk, wv, w1, w2, a1, a2, g1, g2, vecs, seq):
    T, D = h.shape
    tm = RW_ROW_TILE
    assert seq % tm == 0
    full = lambda arr: pl.BlockSpec(arr.shape, lambda i: (0, 0), pipeline_mode=pl.Buffered(1))
    row = pl.BlockSpec((tm, D), lambda i: (i, 0))
    halo = pl.BlockSpec((HALO, D), lambda i: (jnp.maximum(i * (tm // HALO) - 1, 0), 0))
    dtypes = (BF16, F32, BF16, BF16, BF16, BF16, BF16)
    return pl.pallas_call(
        functools.partial(_rwproj_kernel, tm=tm, seq=seq),
        grid=(T // tm,),
        in_specs=[row, halo, full(g), full(mu), full(wr), full(wk), full(wv),
                  full(w1), full(w2), full(a1), full(a2), full(g1), full(g2), full(vecs)],
        out_specs=[row] * 7,
        out_shape=[jax.ShapeDtypeStruct((T, D), dt) for dt in dtypes],
        compiler_params=_params("parallel"),
        name="rwkv_proj",
    )(h, h, g, mu, wr, wk, wv, w1, w2, a1, a2, g1, g2, vecs)


def _bmm(a, b):
    return lax.dot_general(a.astype(BF16), b.astype(BF16), (((2,), (1,)), ((0,), (0,))),
                           preferred_element_type=F32)


def _bmm_nt(a, b):
    return lax.dot_general(a.astype(BF16), b.astype(BF16), (((2,), (2,)), ((0,), (0,))),
                           preferred_element_type=F32)


def _bmm_tn(a, b):
    return lax.dot_general(a.astype(BF16), b.astype(BF16), (((1,), (1,)), ((0,), (0,))),
                           preferred_element_type=F32)


def _rwscan_kernel(r_ref, lw_ref, k_ref, v_ref, kk_ref, a_ref, gate_ref,
                   rk_ref, lng_ref, lnb_ref, o_ref, s_ref, *, L, nsub, npair):
    @pl.when(pl.program_id(2) == 0)
    def _():
        s_ref[...] = jnp.zeros_like(s_ref)

    P = 2 * L
    lane = lax.broadcasted_iota(jnp.int32, (1, 1, LANES), 2)
    ri = lax.broadcasted_iota(jnp.int32, (P, P), 0)
    ci = lax.broadcasted_iota(jnp.int32, (P, P), 1)
    same = (ri >> 6) == (ci >> 6)
    gsum = jnp.where(same, 1.0, 0.0).astype(BF16)
    strict = same & (ci < ri)
    incl = same & (ci <= ri)
    ri_l = lax.broadcasted_iota(jnp.int32, (L, L), 0)
    ci_l = lax.broadcasted_iota(jnp.int32, (L, L), 1)
    tril_l = jnp.where(ci_l <= ri_l, 1.0, 0.0).astype(BF16)

    nb = npair * nsub
    rows = nsub * L

    def pairs(x):
        return jnp.concatenate([x[:, p * LANES:(p + 1) * LANES] for p in range(npair)], axis=0)

    blk = lambda ref: pairs(ref[0].astype(F32)).reshape(nb, L, LANES)
    r, lw, k, v, kk, a = (blk(x) for x in (r_ref, lw_ref, k_ref, v_ref, kk_ref, a_ref))

    def lane_group_sum(x):
        return jnp.dot(x.astype(BF16), gsum, preferred_element_type=F32)

    ssq = lane_group_sum((kk * kk).reshape(nb * L, LANES)).reshape(nb, L, LANES)
    kap = kk * lax.rsqrt(jnp.maximum(ssq, 1e-24))
    bv = kap * a

    t1, t2 = _split(lw)
    tril_b = jnp.broadcast_to(tril_l, (nb, L, L))
    cum = _bmm(tril_b, t1) + _bmm(tril_b, t2)
    c_last = cum[:, L - 1:L, :]
    w_incl = jnp.exp(cum)
    w_excl = jnp.exp(cum - lw)
    w_inv = jnp.exp(-cum)
    w_rem = jnp.exp(c_last - cum)
    w_last = jnp.exp(c_last)

    bf = lambda x: x.astype(BF16)
    head0 = lane < RW_HEAD_DIM
    zb = jnp.zeros((), BF16)
    stack = lambda x: jnp.concatenate([jnp.where(head0, x, zb), jnp.where(head0, zb, x)], axis=1)
    dup = lambda x: jnp.concatenate([x, x], axis=1)
    unstack = lambda x: x[:, :L, :] + x[:, L:, :]

    r_dec = r * w_incl
    a_st = stack(bf(-kap * w_excl))
    r_st = stack(bf(r_dec))
    v_st = stack(bf(v))
    gram = _bmm_nt(jnp.concatenate([a_st, r_st], axis=1),
                   jnp.concatenate([dup(bf(bv * w_inv)), dup(bf(k * w_inv))], axis=1))
    a_ab = bf(jnp.where(strict, gram[:, :P, :P], 0.0))
    a_ak = bf(jnp.where(strict, gram[:, :P, P:], 0.0))
    a_rb = bf(jnp.where(incl, gram[:, P:, :P], 0.0))
    a_rk = bf(jnp.where(incl, gram[:, P:, P:], 0.0))

    tinv = jnp.where(ri == ci, jnp.ones((), BF16), jnp.where((ri >> 1) == (ci >> 1), a_ab, zb))
    n = 2
    while n < L:
        sh = n.bit_length() - 1
        off = ((ri >> (sh + 1)) == (ci >> (sh + 1))) & ((ri >> sh) != (ci >> sh))
        e = jnp.where(off, a_ab, zb)
        tinv = tinv + bf(_bmm(tinv, bf(_bmm(e, tinv))))
        n *= 2

    x1 = bf(_bmm(a_ak, v_st))
    tu = bf(_bmm(tinv, jnp.concatenate([x1, a_st], axis=2)))
    yr = _bmm(a_rb, tu)
    y0 = unstack(yr[:, :, :LANES] + _bmm(a_rk, v_st))
    rq = bf(r_dec + unstack(yr[:, :, LANES:]))
    bh_st = stack(bf(bv * w_rem))
    kh_st = stack(bf(k * w_rem))
    mn = _bmm_tn(bh_st, tu)
    m_bd = bf(mn[:, :, LANES:] + jnp.where(ri == ci, w_last, 0.0))
    n_bd = mn[:, :, :LANES] + _bmm_tn(kh_st, v_st)

    s = [s_ref[p] for p in range(npair)]
    ys = [[None] * nsub for _ in range(npair)]
    for c in range(nsub):
        for p in range(npair):
            i = p * nsub + c
            ys[p][c] = _dot(rq[i], s[p]) + y0[i]
            s[p] = _dot_exact_x(m_bd[i], s[p]) + n_bd[i]
    for p in range(npair):
        s_ref[p] = s[p]
    y = jnp.concatenate([jnp.concatenate(yp, axis=0) for yp in ys], axis=0)

    lanes = lambda x: jnp.concatenate([x[p * rows:(p + 1) * rows] for p in range(npair)], axis=1)
    inv_d = 1.0 / RW_HEAD_DIM
    mean = lane_group_sum(y) * inv_d
    d = y - mean
    var = lane_group_sum(d * d) * inv_d
    yn = lanes(d * lax.rsqrt(var + GN_EPS)) * lng_ref[...] + lnb_ref[...]
    r2, k2, v2 = (x[0].astype(F32) for x in (r_ref, k_ref, v_ref))
    bonus = lanes(lane_group_sum(pairs(r2 * k2 * rk_ref[...]))) * v2
    o_ref[0] = ((yn + bonus) * gate_ref[0].astype(F32)).astype(o_ref.dtype)


def _rwkv_scan(r, lw, k, v, kk, a, gate, r_k, ln_g, ln_b):
    B, S, D = r.shape
    L, lb, npair = RW_CHUNK, RW_BLOCK, RW_PAIRS
    width = npair * LANES
    assert S % lb == 0 and lb % L == 0 and D % width == 0
    blk = pl.BlockSpec((1, lb, width), lambda b, p, j: (b, j, p))
    vec = pl.BlockSpec((1, width), lambda b, p, j: (0, p))
    return pl.pallas_call(
        functools.partial(_rwscan_kernel, L=L, nsub=lb // L, npair=npair),
        grid=(B, D // width, S // lb),
        in_specs=[blk] * 7 + [vec] * 3,
        out_specs=blk,
        out_shape=jax.ShapeDtypeStruct((B, S, D), BF16),
        scratch_shapes=[pltpu.VMEM((npair, LANES, LANES), F32)],
        compiler_params=_params("parallel", "parallel", "arbitrary"),
        name="rwkv_scan",
    )(r, lw, k, v, kk, a, gate, r_k, ln_g, ln_b)


def _even_layer(h, B, S, g, w_in, b_if, conv_w, head_g, w_out, w_up, w_down):
    T, D = h.shape
    ncols = 3 * SB_WIDTH + 4 * ML_WIDTH
    w_main = w_in[:, :ncols].astype(BF16)
    w_if = jnp.pad(w_in[:, ncols:], ((0, 0), (0, LANES - 2 * ML_HEADS))).astype(BF16)
    sb_q, sb_k, sb_v, ml_qk, ml_v, ml_o, ml_if = _even_inproj(h, g[0:1], w_main, w_if)
    to3 = lambda t: t.reshape(B, S, t.shape[-1])
    a_out = _sb_attention(to3(sb_q), to3(sb_k), to3(sb_v))
    gates_nat = to3(ml_if)
    gates_t = jnp.swapaxes(gates_nat[:, :, :2 * ML_HEADS], 1, 2)
    bias_nat = jnp.pad(b_if, (0, LANES - 2 * ML_HEADS)).reshape(1, LANES)
    bias_t = jnp.broadcast_to(b_if[:, None], (2 * ML_HEADS, ML_CHUNK))
    hm = _mlstm(to3(ml_qk), to3(ml_v), to3(ml_o), gates_nat, gates_t, bias_nat, bias_t,
                conv_w, head_g.reshape(ML_HEADS, 1, LANES))
    return _mix_mlp([a_out.reshape(T, SB_WIDTH), hm.reshape(T, ML_WIDTH)],
                    w_out.astype(BF16), h, g, w_up.astype(BF16), w_down.astype(BF16))


def _odd_layer(h, B, S, g, mu, w_rkv, w0, w1, w2, a0, a1, a2, g1, g2, k_k, k_a, r_k,
               ln_g, ln_b, w_out, w_up, w_down):
    T, D = h.shape
    bf = lambda t: t.astype(BF16)
    mu8 = jnp.pad(mu, ((0, HALO - mu.shape[0]), (0, 0)))
    vecs = jnp.pad(jnp.stack([w0, a0, k_k, k_a]), ((0, HALO - 4), (0, 0)))
    r, lw, k, v, kk, a, gate = _rwkv_proj(
        h, g[0:1], mu8, bf(w_rkv[0]), bf(w_rkv[1]), bf(w_rkv[2]),
        bf(w1), bf(w2), bf(a1), bf(a2), bf(g1), bf(g2), vecs, S)
    to3 = lambda t: t.reshape(B, S, D)
    y = _rwkv_scan(to3(r), to3(lw), to3(k), to3(v), to3(kk), to3(a), to3(gate),
                   r_k.reshape(1, D), ln_g.reshape(1, D), ln_b.reshape(1, D))
    return _mix_mlp([y.reshape(T, D)], bf(w_out), h, g, bf(w_up), bf(w_down))


def kernel(x, norm_g, e_w_in, e_b_if, e_conv_w, e_head_g, e_w_out, r_mu, r_w_rkv, r_w0, r_w1, r_w2, r_a0, r_a1, r_a2, r_g1, r_g2, r_k_k, r_k_a, r_r_k, r_ln_g, r_ln_b, r_w_out, mlp_w_up, mlp_w_down):
    B, S, D = x.shape
    h = x.reshape(B * S, D)
    for layer in range(norm_g.shape[0]):
        g = norm_g[layer]
        if layer % 2 == 0:
            e = layer // 2
            h = _even_layer(h, B, S, g, e_w_in[e], e_b_if[e], e_conv_w[e], e_head_g[e],
                            e_w_out[e], mlp_w_up[layer], mlp_w_down[layer])
        else:
            o = layer // 2
            h = _odd_layer(h, B, S, g, r_mu[o], r_w_rkv[o], r_w0[o], r_w1[o], r_w2[o],
                           r_a0[o], r_a1[o], r_a2[o], r_g1[o], r_g2[o], r_k_k[o], r_k_a[o],
                           r_r_k[o], r_ln_g[o], r_ln_b[o], r_w_out[o],
                           mlp_w_up[layer], mlp_w_down[layer])
    return h.reshape(B, S, D)
```

```python
import functools
import math

import jax
import jax.numpy as jnp
from jax import lax
from jax.experimental import pallas as pl
from jax.experimental.pallas import tpu as pltpu

F32 = jnp.float32
BF16 = jnp.bfloat16

LANES = 128
V7X_VMEM_LIMIT_BYTES = 56 * 1024 * 1024

SB_HEADS = 8
SB_HEAD_DIM = 64
SB_WIDTH = SB_HEADS * SB_HEAD_DIM
ML_HEADS = 4
ML_HEAD_DIM = 128
ML_WIDTH = ML_HEADS * ML_HEAD_DIM
CONV_WIDTH = 4
RW_HEAD_DIM = 64
SB_Q_SCALE = -math.log2(math.e) / math.sqrt(SB_HEAD_DIM)
SB_LOG2_MASS_FLOOR = -160.0
NORM_EPS = 1e-6
GN_EPS = 64e-5

ROW_TILE = 1024
RW_ROW_TILE = 512
MLP_ROW_TILE = 1024
MLP_ROW_SUB = 256
MLP_FF_TILE = 1024
SB_TILE = 256
SB_TILES_PER_STEP = 8
ML_CHUNK = 256
ML_CHUNKS_PER_STEP = 4
RW_CHUNK = 64
RW_BLOCK = 512
RW_PAIRS = 8
HALO = 8


def _params(*sem):
    return pltpu.CompilerParams(dimension_semantics=sem,
                                vmem_limit_bytes=V7X_VMEM_LIMIT_BYTES)


def _dot(a, b):
    return jnp.dot(a.astype(BF16), b.astype(BF16), preferred_element_type=F32)


def _split(x):
    hi = x.astype(BF16)
    lo = (x - hi.astype(F32)).astype(BF16)
    return hi, lo


def _dot_x_exact(x, m):
    hi, lo = _split(x)
    return (jnp.dot(hi, m, preferred_element_type=F32)
            + jnp.dot(lo, m, preferred_element_type=F32))


def _dot_exact_x(m, x):
    hi, lo = _split(x)
    return (jnp.dot(m, hi, preferred_element_type=F32)
            + jnp.dot(m, lo, preferred_element_type=F32))


def _rms(x, g):
    ms = jnp.mean(x * x, axis=-1, keepdims=True)
    return x * lax.rsqrt(ms + NORM_EPS) * g


def _softplus(z):
    return jnp.maximum(z, 0.0) + jnp.log(1.0 + jnp.exp(-jnp.abs(z)))


def _neg_abs(x):
    bits = lax.bitcast_convert_type(x, jnp.uint32) | jnp.uint32(0x80000000)
    return lax.bitcast_convert_type(bits, F32)


def _sigmoid(z):
    return 1.0 / (1.0 + jnp.exp(-z))


def _inproj_kernel(x_ref, g_ref, w_ref, wif_ref, *out_refs, widths):
    u = _rms(x_ref[...], g_ref[...]).astype(BF16)
    c0 = 0
    for n, (o_ref, w) in enumerate(zip(out_refs[:-1], widths)):
        t = jnp.dot(u, w_ref[:, c0:c0 + w], preferred_element_type=F32)
        if n == 0:
            t = t * SB_Q_SCALE
        o_ref[...] = t.astype(o_ref.dtype)
        c0 += w
    out_refs[-1][...] = jnp.dot(u, wif_ref[...], preferred_element_type=F32)


def _even_inproj(h, g, w_main, w_if):
    T, D = h.shape
    widths = (SB_WIDTH, SB_WIDTH, SB_WIDTH, 2 * ML_WIDTH, ML_WIDTH, ML_WIDTH)
    tm = ROW_TILE
    assert T % tm == 0
    out_shape = [jax.ShapeDtypeStruct((T, w), BF16) for w in widths]
    out_shape.append(jax.ShapeDtypeStruct((T, LANES), F32))
    out_specs = [pl.BlockSpec((tm, w), lambda i: (i, 0)) for w in widths]
    out_specs.append(pl.BlockSpec((tm, LANES), lambda i: (i, 0)))
    return pl.pallas_call(
        functools.partial(_inproj_kernel, widths=widths),
        grid=(T // tm,),
        in_specs=[pl.BlockSpec((tm, D), lambda i: (i, 0)),
                  pl.BlockSpec((1, D), lambda i: (0, 0)),
                  pl.BlockSpec(w_main.shape, lambda i: (0, 0), pipeline_mode=pl.Buffered(1)),
                  pl.BlockSpec(w_if.shape, lambda i: (0, 0), pipeline_mode=pl.Buffered(1))],
        out_specs=out_specs,
        out_shape=out_shape,
        compiler_params=_params("parallel"),
        name="even_inproj",
    )(h, g, w_main, w_if)


def _sb_kernel(q_ref, k_ref, v_ref, o_ref, *, tq, ntile):
    lane = lax.broadcasted_iota(jnp.int32, (1, LANES), 1)
    first_head = lane < SB_HEAD_DIM
    rr = lax.broadcasted_iota(jnp.int32, (tq, tq), 0)
    cc = lax.broadcasted_iota(jnp.int32, (tq, tq), 1)
    lower = rr > cc
    cum_mat = jnp.where(lower, 1.0, 0.0).astype(BF16)
    heads = (first_head, jnp.logical_not(first_head))

    def weights(qh, ks, carry, diagonal):
        zs = lax.dot_general(qh, ks, (((1,), (1,)), ((), ())), preferred_element_type=F32)
        lk = jnp.minimum(zs, 0.0) - jnp.log2(1.0 + jnp.exp2(_neg_abs(zs)))
        if diagonal:
            lk = jnp.where(lower, lk, 0.0)
        lkb = lk.astype(BF16)
        cs = jnp.dot(lkb, cum_mat, preferred_element_type=F32)
        p = jnp.exp2(lk - zs + cs + carry)
        if diagonal:
            p = jnp.where(lower, p, 0.0)
        return p.astype(BF16), carry + (cs[:, 0:1] + lkb[:, 0:1].astype(F32))

    def span(qhs, start, state, diagonal):
        c0, c1, acc = state
        ks = k_ref[0, pl.ds(start, tq), :]
        vs = v_ref[0, pl.ds(start, tq), :]
        p0, c0 = weights(qhs[0], ks, c0, diagonal)
        p1, c1 = weights(qhs[1], ks, c1, diagonal)
        v01 = jnp.concatenate([jnp.where(keep, vs, jnp.zeros_like(vs)) for keep in heads], axis=0)
        acc = acc + jnp.dot(jnp.concatenate([p0, p1], axis=1), v01, preferred_element_type=F32)
        return c0, c1, acc

    col0 = jnp.zeros((tq, 1), F32)
    zero = (col0, col0, jnp.zeros((tq, LANES), F32))

    def mass_left(st):
        return (jnp.max(jnp.maximum(st[0], st[1])) > SB_LOG2_MASS_FLOOR).astype(jnp.int32)

    for t in range(ntile):
        qi = pl.program_id(2) * ntile + t
        rows = slice(t * tq, (t + 1) * tq)
        q2 = q_ref[0, rows, :]
        qhs = [jnp.where(keep, q2, jnp.zeros_like(q2)) for keep in heads]
        start0 = pl.multiple_of(qi * tq, tq)

        def with_previous_span(_):
            st = span(qhs, start0, zero, True)
            return span(qhs, pl.multiple_of(start0 - tq, tq), st, False)

        def diagonal_only(_):
            return span(qhs, start0, zero, True)

        if t == 0:
            state = lax.cond(qi > 0, with_previous_span, diagonal_only, None)
        else:
            state = with_previous_span(None)

        def cond(c):
            return (c[0] < qi) & (c[1] > 0)

        def body(c):
            j, _, st = c
            st = span(qhs, pl.multiple_of((qi - 1 - j) * tq, tq), st, False)
            return j + 1, mass_left(st), st

        _, _, state = lax.while_loop(cond, body, (jnp.int32(1), mass_left(state), state))
        o_ref[0, rows, :] = state[2].astype(o_ref.dtype)


def _sb_attention(q, k, v):
    B, S, W = q.shape
    tq, ntile = SB_TILE, SB_TILES_PER_STEP
    npair = W // LANES
    rows = tq * ntile
    assert S % rows == 0 and W % LANES == 0
    return pl.pallas_call(
        functools.partial(_sb_kernel, tq=tq, ntile=ntile),
        grid=(B, npair, S // rows),
        in_specs=[pl.BlockSpec((1, rows, LANES), lambda b, p, i: (b, i, p)),
                  pl.BlockSpec((1, S, LANES), lambda b, p, i: (b, 0, p)),
                  pl.BlockSpec((1, S, LANES), lambda b, p, i: (b, 0, p))],
        out_specs=pl.BlockSpec((1, rows, LANES), lambda b, p, i: (b, i, p)),
        out_shape=jax.ShapeDtypeStruct((B, S, W), BF16),
        compiler_params=_params("parallel", "parallel", "arbitrary"),
        name="sb_attention",
    )(q, k, v)


def _mlstm_kernel(q_ref, k_ref, v_ref, o_ref, gn_ref, gt_ref, bn_ref, bt_ref,
                  cwq_ref, cwk_ref, hg_ref, out_ref,
                  c_ref, m_ref, pq_ref, pk_ref, *, L, first):
    if first is not None:
        @pl.when(first)
        def _():
            c_ref[...] = jnp.zeros_like(c_ref)
            m_ref[...] = jnp.zeros_like(m_ref)
            pq_ref[...] = jnp.zeros_like(pq_ref)
            pk_ref[...] = jnp.zeros_like(pk_ref)

    H = ML_HEADS
    heads = lambda x: jnp.stack([x[:, h * LANES:(h + 1) * LANES] for h in range(H)])
    unheads = lambda x: jnp.concatenate([x[h] for h in range(H)], axis=1)

    def conv_silu(x_ref, prev_ref, w):
        x = x_ref[0].astype(F32)
        xf = jnp.concatenate([prev_ref[...], x], axis=0)
        prev_ref[...] = x[L - HALO:, :]
        y = xf * w[CONV_WIDTH - 1:CONV_WIDTH, :]
        for j in range(1, CONV_WIDTH):
            y = y + pltpu.roll(xf, j, 0) * w[CONV_WIDTH - 1 - j:CONV_WIDTH - j, :]
        y = y[HALO:, :]
        return heads(y * _sigmoid(y))

    q = conv_silu(q_ref, pq_ref, cwq_ref[...])
    k = conv_silu(k_ref, pk_ref, cwk_ref[...]) * (ML_HEAD_DIM ** -0.5)
    v_aug = jnp.concatenate([heads(v_ref[0]), jnp.ones((H, L, LANES), BF16)], axis=2)

    ri = lax.broadcasted_iota(jnp.int32, (L, L), 0)
    ci = lax.broadcasted_iota(jnp.int32, (L, L), 1)
    tril = ci <= ri
    tril_m = jnp.where(tril, 1.0, 0.0).astype(BF16)
    triu_m = jnp.where(ci >= ri, 1.0, 0.0).astype(BF16)
    gn = gn_ref[0] + bn_ref[...]
    gt = gt_ref[0] + bt_ref[...]
    col = lambda x, n: jnp.stack([x[:, n + h:n + h + 1] for h in range(H)])
    row = lambda x, n: jnp.stack([x[n + h:n + h + 1, :] for h in range(H)])
    li_col = col(gn, 0)
    lf_col = -_softplus(-col(gn, H))
    li_row = row(gt, 0)
    lf_all = -_softplus(-gt)
    lf_wide = jnp.concatenate([jnp.broadcast_to(lf_col[h], (L, LANES)) for h in range(H)], axis=1)
    b_col = heads(_dot_exact_x(tril_m, lf_wide))[:, :, :1]
    b_row = row(_dot_x_exact(lf_all, triu_m), H)

    m_prev = m_ref[:, 0:1, 0:1]
    dmat = jnp.where(tril, b_col - b_row + li_row, -jnp.inf)
    inter = b_col + m_prev
    m_t = jnp.maximum(jnp.max(dmat, axis=2, keepdims=True), inter)
    scores = _bmm_nt(q, k) * jnp.exp(dmat - m_t)
    w_inter = jnp.exp(inter - m_t)
    c_aug = c_ref[...]
    num_aug = _bmm(scores, v_aug) + w_inter * _bmm(q, c_aug)
    num = num_aug[:, :, :LANES]
    den = num_aug[:, :, LANES:LANES + 1]
    hval = num / jnp.maximum(jnp.abs(den), jnp.exp(-m_t))

    b_last = b_col[:, L - 1:L, :]
    gcol = b_last - b_col + li_col
    m_new = jnp.maximum(b_last + m_prev, jnp.max(gcol, axis=1, keepdims=True))
    w_state = jnp.exp(b_last + m_prev - m_new)
    w_tok = jnp.exp(gcol - m_new)
    c_ref[...] = w_state * c_aug + _bmm_tn(k * w_tok, v_aug)
    m_ref[...] = jnp.broadcast_to(m_new, m_ref.shape)

    hn = hval * lax.rsqrt(jnp.mean(hval * hval, axis=-1, keepdims=True) + NORM_EPS) * hg_ref[...]
    out_ref[0] = (unheads(hn) * _sigmoid(o_ref[0].astype(F32))).astype(out_ref.dtype)


def _mlstm_step_kernel(q_ref, k_ref, v_ref, o_ref, gn_ref, gt_ref, *rest, L, nchunk):
    for s in range(nchunk):
        rows = pl.ds(s * L, L)
        sub = lambda ref: ref.at[:, rows, :]
        *consts, out_ref, c_ref, m_ref, pq_ref, pk_ref = rest
        _mlstm_kernel(sub(q_ref), sub(k_ref), sub(v_ref), sub(o_ref), sub(gn_ref),
                      gt_ref.at[:, :, rows], *consts, sub(out_ref), c_ref, m_ref, pq_ref, pk_ref,
                      L=L, first=(pl.program_id(1) == 0) if s == 0 else None)


def _mlstm(qk, v, o, gates_nat, gates_t, bias_nat, bias_t, conv_w, head_g):
    B, S, W = v.shape
    L, nchunk = ML_CHUNK, ML_CHUNKS_PER_STEP
    rows = L * nchunk
    assert S % rows == 0
    blk = lambda off: pl.BlockSpec((1, rows, W), lambda b, c: (b, c, off))
    full = lambda arr: pl.BlockSpec(arr.shape, lambda b, c: (0,) * arr.ndim)
    return pl.pallas_call(
        functools.partial(_mlstm_step_kernel, L=L, nchunk=nchunk),
        grid=(B, S // rows),
        in_specs=[blk(0), blk(1), blk(0), blk(0),
                  pl.BlockSpec((1, rows, LANES), lambda b, c: (b, c, 0)),
                  pl.BlockSpec((1, 2 * ML_HEADS, rows), lambda b, c: (b, 0, c)),
                  full(bias_nat), full(bias_t),
                  pl.BlockSpec((CONV_WIDTH, W), lambda b, c: (0, 0)),
                  pl.BlockSpec((CONV_WIDTH, W), lambda b, c: (0, 1)),
                  full(head_g)],
        out_specs=pl.BlockSpec((1, rows, W), lambda b, c: (b, c, 0)),
        out_shape=jax.ShapeDtypeStruct((B, S, W), BF16),
        scratch_shapes=[pltpu.VMEM((ML_HEADS, ML_HEAD_DIM, 2 * LANES), F32),
                        pltpu.VMEM((ML_HEADS, HALO, LANES), F32),
                        pltpu.VMEM((HALO, W), F32),
                        pltpu.VMEM((HALO, W), F32)],
        compiler_params=_params("parallel", "arbitrary"),
        name="mlstm",
    )(qk, qk, v, o, gates_nat, gates_t, bias_nat, bias_t, conv_w, conv_w, head_g)


def _mixmlp_kernel(*refs, nparts):
    parts = refs[:nparts]
    wo_ref, h_ref, g_ref, wu_ref, wd_ref, o_ref, u_ref = refs[nparts:]
    sub = MLP_ROW_SUB
    for s in range(h_ref.shape[0] // sub):
        rows = slice(s * sub, (s + 1) * sub)
        c0 = 0
        mix = None
        for p in parts:
            w = p.shape[1]
            t = jnp.dot(p[rows, :], wo_ref[c0:c0 + w, :], preferred_element_type=F32)
            mix = t if mix is None else mix + t
            c0 += w
        h1 = h_ref[rows, :] + _rms(mix, g_ref[1:2, :])
        o_ref[rows, :] = h1
        u_ref[rows, :] = _rms(h1, g_ref[2:3, :]).astype(BF16)

    u = u_ref[...]
    tf = MLP_FF_TILE
    acc = None
    for c in range(wu_ref.shape[1] // tf):
        a = jnp.maximum(jnp.dot(u, wu_ref[:, c * tf:(c + 1) * tf], preferred_element_type=F32), 0.0)
        t = jnp.dot((a * a).astype(BF16), wd_ref[c * tf:(c + 1) * tf, :], preferred_element_type=F32)
        acc = t if acc is None else acc + t
    o_ref[...] = o_ref[...] + _rms(acc, g_ref[3:4, :])


def _mix_mlp(parts, w_out, h, g, w_up, w_down):
    T, D = h.shape
    tm = MLP_ROW_TILE
    assert T % tm == 0 and tm % MLP_ROW_SUB == 0 and w_up.shape[1] % MLP_FF_TILE == 0
    once = lambda arr: pl.BlockSpec(arr.shape, lambda i: (0, 0), pipeline_mode=pl.Buffered(1))
    in_specs = [pl.BlockSpec((tm, p.shape[1]), lambda i: (i, 0)) for p in parts]
    in_specs += [once(w_out), pl.BlockSpec((tm, D), lambda i: (i, 0)), once(g),
                 once(w_up), once(w_down)]
    return pl.pallas_call(
        functools.partial(_mixmlp_kernel, nparts=len(parts)),
        grid=(T // tm,),
        in_specs=in_specs,
        out_specs=pl.BlockSpec((tm, D), lambda i: (i, 0)),
        out_shape=jax.ShapeDtypeStruct((T, D), F32),
        scratch_shapes=[pltpu.VMEM((tm, D), BF16)],
        compiler_params=_params("parallel"),
        name="mix_mlp",
    )(*parts, w_out, h, g, w_up, w_down)


def _rwproj_kernel(x_ref, xh_ref, g_ref, mu_ref, wr_ref, wk_ref, wv_ref,
                   w1_ref, w2_ref, a1_ref, a2_ref, g1_ref, g2_ref, vec_ref,
                   r_ref, lw_ref, k_ref, v_ref, kk_ref, a_ref, gate_ref, *, tm, seq):
    i = pl.program_id(0)
    g = g_ref[...]
    w0, a0, k_k, k_a = (vec_ref[n:n + 1, :] for n in range(4))
    not_first = jnp.where((i * tm) % seq == 0, 0.0, 1.0)
    up = _rms(xh_ref[HALO - 1:HALO, :], g) * not_first
    u = _rms(x_ref[...], g)
    row = lax.broadcasted_iota(jnp.int32, (tm, 1), 0)
    x_prev = jnp.where(row == 0, up, pltpu.roll(u, 1, 0))
    ub = u.astype(BF16)
    xxb = (x_prev - u).astype(BF16)
    mub = mu_ref[...].astype(BF16)
    mix = lambda n: ub + xxb * mub[n:n + 1, :]

    r = jnp.dot(mix(0), wr_ref[...], preferred_element_type=F32)
    k = jnp.dot(mix(2), wk_ref[...], preferred_element_type=F32)
    v = jnp.dot(mix(3), wv_ref[...], preferred_element_type=F32)
    dw = _dot(jnp.tanh(jnp.dot(mix(1), w1_ref[...], preferred_element_type=F32)), w2_ref[...])
    a = _sigmoid(a0 + _dot(jnp.dot(mix(4), a1_ref[...], preferred_element_type=F32), a2_ref[...]))
    gate = _dot(_sigmoid(jnp.dot(mix(5), g1_ref[...], preferred_element_type=F32)), g2_ref[...])

    r_ref[...] = r.astype(r_ref.dtype)
    lw_ref[...] = (-math.exp(-0.5)) * _sigmoid(w0 + dw)
    k_ref[...] = (k * (1.0 + (a - 1.0) * k_a)).astype(k_ref.dtype)
    v_ref[...] = v.astype(v_ref.dtype)
    kk_ref[...] = (k * k_k).astype(kk_ref.dtype)
    a_ref[...] = a.astype(a_ref.dtype)
    gate_ref[...] = gate.astype(gate_ref.dtype)


def _rwkv_proj(h, g, mu, wr, wk, wv, w1, w2, a1, a2, g1, g2, vecs, seq):
    T, D = h.shape
    tm = RW_ROW_TILE
    assert seq % tm == 0
    full = lambda arr: pl.BlockSpec(arr.shape, lambda i: (0, 0), pipeline_mode=pl.Buffered(1))
    row = pl.BlockSpec((tm, D), lambda i: (i, 0))
    halo = pl.BlockSpec((HALO, D), lambda i: (jnp.maximum(i * (tm // HALO) - 1, 0), 0))
    dtypes = (BF16, F32, BF16, BF16, BF16, BF16, BF16)
    return pl.pallas_call(
        functools.partial(_rwproj_kernel, tm=tm, seq=seq),
        grid=(T // tm,),
        in_specs=[row, halo, full(g), full(mu), full(wr), full(wk), full(wv),
                  full(w1), full(w2), full(a1), full(a2), full(g1), full(g2), full(vecs)],
        out_specs=[row] * 7,
        out_shape=[jax.ShapeDtypeStruct((T, D), dt) for dt in dtypes],
        compiler_params=_params("parallel"),
        name="rwkv_proj",
    )(h, h, g, mu, wr, wk, wv, w1, w2, a1, a2, g1, g2, vecs)


def _bmm(a, b):
    return lax.dot_general(a.astype(BF16), b.astype(BF16), (((2,), (1,)), ((0,), (0,))),
                           preferred_element_type=F32)


def _bmm_nt(a, b):
    return lax.dot_general(a.astype(BF16), b.astype(BF16), (((2,), (2,)), ((0,), (0,))),
                           preferred_element_type=F32)


def _bmm_tn(a, b):
    return lax.dot_general(a.astype(BF16), b.astype(BF16), (((1,), (1,)), ((0,), (0,))),
                           preferred_element_type=F32)


def _rwscan_kernel(r_ref, lw_ref, k_ref, v_ref, kk_ref, a_ref, gate_ref,
                   rk_ref, lng_ref, lnb_ref, o_ref, s_ref, *, L, nsub, npair):
    @pl.when(pl.program_id(2) == 0)
    def _():
        s_ref[...] = jnp.zeros_like(s_ref)

    P = 2 * L
    lane = lax.broadcasted_iota(jnp.int32, (1, 1, LANES), 2)
    ri = lax.broadcasted_iota(jnp.int32, (P, P), 0)
    ci = lax.broadcasted_iota(jnp.int32, (P, P), 1)
    same = (ri >> 6) == (ci >> 6)
    gsum = jnp.where(same, 1.0, 0.0).astype(BF16)
    strict = same & (ci < ri)
    incl = same & (ci <= ri)
    ri_l = lax.broadcasted_iota(jnp.int32, (L, L), 0)
    ci_l = lax.broadcasted_iota(jnp.int32, (L, L), 1)
    tril_l = jnp.where(ci_l <= ri_l, 1.0, 0.0).astype(BF16)

    nb = npair * nsub
    rows = nsub * L

    def pairs(x):
        return jnp.concatenate([x[:, p * LANES:(p + 1) * LANES] for p in range(npair)], axis=0)

    blk = lambda ref: pairs(ref[0].astype(F32)).reshape(nb, L, LANES)
    r, lw, k, v, kk, a = (blk(x) for x in (r_ref, lw_ref, k_ref, v_ref, kk_ref, a_ref))

    def lane_group_sum(x):
        return jnp.dot(x.astype(BF16), gsum, preferred_element_type=F32)

    ssq = lane_group_sum((kk * kk).reshape(nb * L, LANES)).reshape(nb, L, LANES)
    kap = kk * lax.rsqrt(jnp.maximum(ssq, 1e-24))
    bv = kap * a

    t1, t2 = _split(lw)
    tril_b = jnp.broadcast_to(tril_l, (nb, L, L))
    cum = _bmm(tril_b, t1) + _bmm(tril_b, t2)
    c_last = cum[:, L - 1:L, :]
    w_incl = jnp.exp(cum)
    w_excl = jnp.exp(cum - lw)
    w_inv = jnp.exp(-cum)
    w_rem = jnp.exp(c_last - cum)
    w_last = jnp.exp(c_last)

    bf = lambda x: x.astype(BF16)
    head0 = lane < RW_HEAD_DIM
    zb = jnp.zeros((), BF16)
    stack = lambda x: jnp.concatenate([jnp.where(head0, x, zb), jnp.where(head0, zb, x)], axis=1)
    dup = lambda x: jnp.concatenate([x, x], axis=1)
    unstack = lambda x: x[:, :L, :] + x[:, L:, :]

    r_dec = r * w_incl
    a_st = stack(bf(-kap * w_excl))
    r_st = stack(bf(r_dec))
    v_st = stack(bf(v))
    gram = _bmm_nt(jnp.concatenate([a_st, r_st], axis=1),
                   jnp.concatenate([dup(bf(bv * w_inv)), dup(bf(k * w_inv))], axis=1))
    a_ab = bf(jnp.where(strict, gram[:, :P, :P], 0.0))
    a_ak = bf(jnp.where(strict, gram[:, :P, P:], 0.0))
    a_rb = bf(jnp.where(incl, gram[:, P:, :P], 0.0))
    a_rk = bf(jnp.where(incl, gram[:, P:, P:], 0.0))

    tinv = jnp.where(ri == ci, jnp.ones((), BF16), jnp.where((ri >> 1) == (ci >> 1), a_ab, zb))
    n = 2
    while n < L:
        sh = n.bit_length() - 1
        off = ((ri >> (sh + 1)) == (ci >> (sh + 1))) & ((ri >> sh) != (ci >> sh))
        e = jnp.where(off, a_ab, zb)
        tinv = tinv + bf(_bmm(tinv, bf(_bmm(e, tinv))))
        n *= 2

    x1 = bf(_bmm(a_ak, v_st))
    tu = bf(_bmm(tinv, jnp.concatenate([x1, a_st], axis=2)))
    yr = _bmm(a_rb, tu)
    y0 = unstack(yr[:, :, :LANES] + _bmm(a_rk, v_st))
    rq = bf(r_dec + unstack(yr[:, :, LANES:]))
    bh_st = stack(bf(bv * w_rem))
    kh_st = stack(bf(k * w_rem))
    mn = _bmm_tn(bh_st, tu)
    m_bd = bf(mn[:, :, LANES:] + jnp.where(ri == ci, w_last, 0.0))
    n_bd = mn[:, :, :LANES] + _bmm_tn(kh_st, v_st)

    s = [s_ref[p] for p in range(npair)]
    ys = [[None] * nsub for _ in range(npair)]
    for c in range(nsub):
        for p in range(npair):
            i = p * nsub + c
            ys[p][c] = _dot(rq[i], s[p]) + y0[i]
            s[p] = _dot_exact_x(m_bd[i], s[p]) + n_bd[i]
    for p in range(npair):
        s_ref[p] = s[p]
    y = jnp.concatenate([jnp.concatenate(yp, axis=0) for yp in ys], axis=0)

    lanes = lambda x: jnp.concatenate([x[p * rows:(p + 1) * rows] for p in range(npair)], axis=1)
    inv_d = 1.0 / RW_HEAD_DIM
    mean = lane_group_sum(y) * inv_d
    d = y - mean
    var = lane_group_sum(d * d) * inv_d
    yn = lanes(d * lax.rsqrt(var + GN_EPS)) * lng_ref[...] + lnb_ref[...]
    r2, k2, v2 = (x[0].astype(F32) for x in (r_ref, k_ref, v_ref))
    bonus = lanes(lane_group_sum(pairs(r2 * k2 * rk_ref[...]))) * v2
    o_ref[0] = ((yn + bonus) * gate_ref[0].astype(F32)).astype(o_ref.dtype)


def _rwkv_scan(r, lw, k, v, kk, a, gate, r_k, ln_g, ln_b):
    B, S, D = r.shape
    L, lb, npair = RW_CHUNK, RW_BLOCK, RW_PAIRS
    width = npair * LANES
    assert S % lb == 0 and lb % L == 0 and D % width == 0
    blk = pl.BlockSpec((1, lb, width), lambda b, p, j: (b, j, p))
    vec = pl.BlockSpec((1, width), lambda b, p, j: (0, p))
    return pl.pallas_call(
        functools.partial(_rwscan_kernel, L=L, nsub=lb // L, npair=npair),
        grid=(B, D // width, S // lb),
        in_specs=[blk] * 7 + [vec] * 3,
        out_specs=blk,
        out_shape=jax.ShapeDtypeStruct((B, S, D), BF16),
        scratch_shapes=[pltpu.VMEM((npair, LANES, LANES), F32)],
        compiler_params=_params("parallel", "parallel", "arbitrary"),
        name="rwkv_scan",
    )(r, lw, k, v, kk, a, gate, r_k, ln_g, ln_b)


def _even_layer(h, B, S, g, w_in, b_if, conv_w, head_g, w_out, w_up, w_down):
    T, D = h.shape
    ncols = 3 * SB_WIDTH + 4 * ML_WIDTH
    w_main = w_in[:, :ncols].astype(BF16)
    w_if = jnp.pad(w_in[:, ncols:], ((0, 0), (0, LANES - 2 * ML_HEADS))).astype(BF16)
    sb_q, sb_k, sb_v, ml_qk, ml_v, ml_o, ml_if = _even_inproj(h, g[0:1], w_main, w_if)
    to3 = lambda t: t.reshape(B, S, t.shape[-1])
    a_out = _sb_attention(to3(sb_q), to3(sb_k), to3(sb_v))
    gates_nat = to3(ml_if)
    gates_t = jnp.swapaxes(gates_nat[:, :, :2 * ML_HEADS], 1, 2)
    bias_nat = jnp.pad(b_if, (0, LANES - 2 * ML_HEADS)).reshape(1, LANES)
    bias_t = jnp.broadcast_to(b_if[:, None], (2 * ML_HEADS, ML_CHUNK))
    hm = _mlstm(to3(ml_qk), to3(ml_v), to3(ml_o), gates_nat, gates_t, bias_nat, bias_t,
                conv_w, head_g.reshape(ML_HEADS, 1, LANES))
    return _mix_mlp([a_out.reshape(T, SB_WIDTH), hm.reshape(T, ML_WIDTH)],
                    w_out.astype(BF16), h, g, w_up.astype(BF16), w_down.astype(BF16))


def _odd_layer(h, B, S, g, mu, w_rkv, w0, w1, w2, a0, a1, a2, g1, g2, k_k, k_a, r_k,
               ln_g, ln_b, w_out, w_up, w_down):
    T, D = h.shape
    bf = lambda t: t.astype(BF16)
    mu8 = jnp.pad(mu, ((0, HALO - mu.shape[0]), (0, 0)))
    vecs = jnp.pad(jnp.stack([w0, a0, k_k, k_a]), ((0, HALO - 4), (0, 0)))
    r, lw, k, v, kk, a, gate = _rwkv_proj(
        h, g[0:1], mu8, bf(w_rkv[0]), bf(w_rkv[1]), bf(w_rkv[2]),
        bf(w1), bf(w2), bf(a1), bf(a2), bf(g1), bf(g2), vecs, S)
    to3 = lambda t: t.reshape(B, S, D)
    y = _rwkv_scan(to3(r), to3(lw), to3(k), to3(v), to3(kk), to3(a), to3(gate),
                   r_k.reshape(1, D), ln_g.reshape(1, D), ln_b.reshape(1, D))
    return _mix_mlp([y.reshape(T, D)], bf(w_out), h, g, bf(w_up), bf(w_down))


def kernel(x, norm_g, e_w_in, e_b_if, e_conv_w, e_head_g, e_w_out, r_mu, r_w_rkv, r_w0, r_w1, r_w2, r_a0, r_a1, r_a2, r_g1, r_g2, r_k_k, r_k_a, r_r_k, r_ln_g, r_ln_b, r_w_out, mlp_w_up, mlp_w_down):
    B, S, D = x.shape
    h = x.reshape(B * S, D)
    for layer in range(norm_g.shape[0]):
        g = norm_g[layer]
        if layer % 2 == 0:
            e = layer // 2
            h = _even_layer(h, B, S, g, e_w_in[e], e_b_if[e], e_conv_w[e], e_head_g[e],
                            e_w_out[e], mlp_w_up[layer], mlp_w_down[layer])
        else:
            o = layer // 2
            h = _odd_layer(h, B, S, g, r_mu[o], r_w_rkv[o], r_w0[o], r_w1[o], r_w2[o],
                           r_a0[o], r_a1[o], r_a2[o], r_g1[o], r_g2[o], r_k_k[o], r_k_a[o],
                           r_r_k[o], r_ln_g[o], r_ln_b[o], r_w_out[o],
                           mlp_w_up[layer], mlp_w_down[layer])
    return h.reshape(B, S, D)
```
